```python
import jax, jax.numpy as jnp
from jax import lax
import numpy as np

D_MODEL = 2048
BATCH = 4
SEQ = 4096
DEPTH = 1
DEC_BATCH = 4
DEC_SEQ = 2048
PAST_LEN = 128

N_MEM = 256
EPS = 1e-5

GLA_HEADS = 4
GLA_V_W = D_MODEL // 2
GLA_QK_W = GLA_V_W // 2
GLA_DK = GLA_QK_W // GLA_HEADS
GLA_DV = GLA_V_W // GLA_HEADS
GLA_GATE_RANK = 16
GLA_GATE_NORM = 16.0
GLA_CHUNK = 64

SWA_HEADS = 8
SWA_KV_HEADS = 4
SWA_HEAD_DIM = 128
SWA_WINDOW = 128
SWA_BLOCK = 128
ROPE_THETA = 500000.0
ROPE_DIM = SWA_HEAD_DIM // 4

CROSS_HEADS = 4
CROSS_HEAD_DIM = 128

N_EXPERTS = 32
TOP_K = 4
D_FF = D_MODEL // 8
SWIGLU_LIMIT = 7.0
SWIGLU_ALPHA = 1.702

SWA_Q_W = SWA_HEADS * SWA_HEAD_DIM
SWA_KV_W = SWA_KV_HEADS * SWA_HEAD_DIM
CROSS_W = CROSS_HEADS * CROSS_HEAD_DIM
IN_SPLITS = (GLA_QK_W, GLA_QK_W, GLA_V_W, GLA_V_W, 2 * GLA_GATE_RANK, SWA_Q_W, SWA_KV_W, SWA_KV_W, 2 * D_MODEL)
IN_COLS = 2 * GLA_QK_W + 2 * GLA_V_W + 2 * GLA_GATE_RANK + SWA_Q_W + 2 * SWA_KV_W + 2 * D_MODEL

kernel_name = "hybrid_gla_swa_moe_encoder"


def rms_norm(x, g):
    xf = x.astype(jnp.float32)
    y = xf * lax.rsqrt(jnp.mean(xf * xf, axis=-1, keepdims=True) + EPS)
    return (y * g.astype(jnp.float32)).astype(x.dtype)


def split_columns(t, sizes):
    out = []
    start = 0
    for s in sizes:
        out.append(t[..., start:start + s])
        start += s
    return out


def partial_rope(x, pos):
    half = ROPE_DIM // 2
    inv = ROPE_THETA ** (-jnp.arange(half, dtype=jnp.float32) / half)
    ang = pos.astype(jnp.float32)[:, None] * inv[None, :]
    cos = jnp.cos(ang)[:, None, :]
    sin = jnp.sin(ang)[:, None, :]
    xr = x[..., :ROPE_DIM].astype(jnp.float32)
    x1, x2 = xr[..., :half], xr[..., half:]
    rot = jnp.concatenate([x1 * cos - x2 * sin, x2 * cos + x1 * sin], axis=-1).astype(x.dtype)
    return jnp.concatenate([rot, x[..., ROPE_DIM:]], axis=-1)


def gla_chunked(q, k, v, log_a, strict):
    B, H, S, dk = q.shape
    dv = v.shape[-1]
    C = GLA_CHUNK
    N = S // C
    q = q.reshape(B, H, N, C, dk)
    k = k.reshape(B, H, N, C, dk)
    v = v.reshape(B, H, N, C, dv)
    b = jnp.cumsum(log_a.reshape(B, H, N, C, dk), axis=3)
    b_last = b[:, :, :, -1:, :]
    q_dec = q * jnp.exp(b)
    k_dec = k * jnp.exp(-b)
    k_end = k * jnp.exp(b_last - b)
    mask = jnp.tril(jnp.ones((C, C), dtype=bool), k=-1 if strict else 0)
    att = jnp.where(mask, jnp.einsum('bhncd,bhnsd->bhncs', q_dec, k_dec), 0.0)
    o_intra = jnp.einsum('bhncs,bhnsv->bhncv', att, v)
    decay = jnp.exp(b_last[:, :, :, 0, :])

    def step(state, inp):
        q_n, k_n, v_n, d_n = inp
        o_n = jnp.einsum('bhcd,bhdv->bhcv', q_n, state)
        state = d_n[..., None] * state + jnp.einsum('bhcd,bhcv->bhdv', k_n, v_n)
        return state, o_n

    to_scan = lambda t: jnp.moveaxis(t, 2, 0)
    state0 = jnp.zeros((B, H, dk, dv), q.dtype)
    _, o_inter = lax.scan(step, state0, (to_scan(q_dec), to_scan(k_end), to_scan(v), to_scan(decay)))
    o = o_intra + jnp.moveaxis(o_inter, 0, 2)
    return o.reshape(B, H, S, dv)


def gla_branch(q, k, v, r, gate_lr, w_dec, b_dec, g_out):
    B, S, _ = q.shape
    heads = lambda t, d: t.reshape(B, S, GLA_HEADS, d).transpose(0, 2, 1, 3).astype(jnp.float32)
    qh = heads(q, GLA_DK) * (GLA_DK ** -0.5)
    kh = heads(k, GLA_DK)
    vh = heads(v, GLA_DV)
    lr = gate_lr.astype(jnp.float32).reshape(B, S, 2, GLA_GATE_RANK)
    z = jnp.einsum('bsdr,drk->bsdk', lr, w_dec.astype(jnp.float32)) + b_dec.astype(jnp.float32)
    log_a = jax.nn.log_sigmoid(z) / GLA_GATE_NORM
    la_f = heads(log_a[:, :, 0], GLA_DK)
    la_b = heads(log_a[:, :, 1], GLA_DK)
    flip = lambda t: jnp.flip(t, axis=2)
    o_f = gla_chunked(qh, kh, vh, la_f, strict=False)
    o_b = flip(gla_chunked(flip(qh), flip(kh), flip(vh), flip(la_b), strict=True))
    o = o_f + o_b
    o = o * lax.rsqrt(jnp.mean(o * o, axis=-1, keepdims=True) + EPS)
    o = o.transpose(0, 2, 1, 3).reshape(B, S, GLA_V_W) * g_out.astype(jnp.float32)
    return (o * jax.nn.silu(r.astype(jnp.float32))).astype(q.dtype)


def swa_branch(q, k, v, sink, pos):
    B, S, _ = q.shape
    W = SWA_BLOCK
    N = S // W
    G = SWA_HEADS // SWA_KV_HEADS
    hd = SWA_HEAD_DIM
    q = partial_rope(q.reshape(B, S, SWA_HEADS, hd), pos)
    k = partial_rope(k.reshape(B, S, SWA_KV_HEADS, hd), pos)
    v = v.reshape(B, S, SWA_KV_HEADS, hd)
    qb = q.reshape(B, N, W, SWA_KV_HEADS, G, hd)

    def neighbours(t):
        tp = jnp.pad(t, ((0, 0), (W, W), (0, 0), (0, 0))).reshape(B, N + 2, W, SWA_KV_HEADS, hd)
        return jnp.concatenate([tp[:, :-2], tp[:, 1:-1], tp[:, 2:]], axis=2)

    kb = neighbours(k)
    vb = neighbours(v)
    s = jnp.einsum('bnqkgd,bnskd->bnkgqs', qb, kb).astype(jnp.float32) * (hd ** -0.5)
    qi = jnp.arange(W)[:, None]
    kj = jnp.arange(3 * W)[None, :]
    in_window = jnp.abs(kj - W - qi) <= SWA_WINDOW
    kpos = (jnp.arange(N)[:, None] - 1) * W + jnp.arange(3 * W)[None, :]
    in_seq = (kpos >= 0) & (kpos < S)
    valid = in_window[None] & in_seq[:, None, :]
    s = jnp.where(valid[None, :, None, None], s, -jnp.inf)
    sink_l = jnp.broadcast_to(sink.astype(jnp.float32).reshape(SWA_KV_HEADS, G)[None, None, :, :, None, None],
                              s.shape[:-1] + (1,))
    p = jax.nn.softmax(jnp.concatenate([s, sink_l], axis=-1), axis=-1)[..., :-1]
    o = jnp.einsum('bnkgqs,bnskd->bnqkgd', p.astype(v.dtype), vb)
    return o.reshape(B, S, SWA_Q_W)


def parallel_mixer(x, pos, g_norm, w_in, w_dec, b_dec, g_gla, sink, b_merge, w_proj_gla, w_proj_swa, w_out):
    B, S, D = x.shape
    h = rms_norm(x, g_norm)
    proj = h @ w_in
    gq, gk, gv, gr, glr, sq, sk, sv, gates = split_columns(proj, IN_SPLITS)
    o_gla = gla_branch(gq, gk, gv, gr, glr, w_dec, b_dec, g_gla) @ w_proj_gla
    o_swa = swa_branch(sq, sk, sv, sink, pos) @ w_proj_swa
    gt = jax.nn.sigmoid(gates.astype(jnp.float32).reshape(B, S, 2, D) + b_merge.astype(jnp.float32))
    merged = (gt[:, :, 0] * o_gla + gt[:, :, 1] * o_swa).astype(x.dtype)
    return merged @ w_out


def cross_attention(x, mem, g_x, g_mem, w_q, w_kv, w_o):
    B, S, _ = x.shape
    h = rms_norm(x, g_x)
    m = rms_norm(mem, g_mem)
    q = (h @ w_q).reshape(B, S, CROSS_HEADS, CROSS_HEAD_DIM)
    kv = (m @ w_kv).reshape(B, N_MEM, 2, CROSS_HEADS, CROSS_HEAD_DIM)
    k, v = kv[:, :, 0], kv[:, :, 1]
    s = jnp.einsum('bqhd,bmhd->bhqm', q, k).astype(jnp.float32) * (CROSS_HEAD_DIM ** -0.5)
    p = jax.nn.softmax(s, axis=-1)
    o = jnp.einsum('bhqm,bmhd->bqhd', p.astype(v.dtype), v).reshape(B, S, CROSS_W)
    return o @ w_o


def moe(x, g, w_router, b_router, w_up, b_up, w_down, b_down):
    B, S, D = x.shape
    T = B * S
    h = rms_norm(x, g).reshape(T, D)
    logits = (h @ w_router).astype(jnp.float32) + b_router.astype(jnp.float32)
    top_val, top_idx = lax.top_k(logits, TOP_K)
    gates = jax.nn.softmax(top_val, axis=-1)
    combine = jnp.sum(jax.nn.one_hot(top_idx, N_EXPERTS, dtype=jnp.float32) * gates[..., None], axis=1)
    up = jnp.einsum('td,edf->tef', h, w_up) + b_up
    glu = jnp.minimum(up[..., 0::2], SWIGLU_LIMIT)
    lin = jnp.clip(up[..., 1::2], -SWIGLU_LIMIT, SWIGLU_LIMIT)
    act = (glu * jax.nn.sigmoid(SWIGLU_ALPHA * glu) * (lin + 1.0)) * combine[..., None].astype(up.dtype)
    y = jnp.einsum('tef,efd->td', act.astype(h.dtype), w_down) + (combine.astype(h.dtype) @ b_down)
    return y.astype(x.dtype).reshape(B, S, D)


def encoder_trunk(x, mem, norm_mix, w_in, w_gla_decay, b_gla_decay, gla_out_norm, swa_sink, b_merge,
                  w_proj_gla, w_proj_swa, w_out, norm_cross, norm_mem, w_cross_q, w_cross_kv, w_cross_o,
                  norm_moe, w_router, b_router, w_up, b_up, w_down, b_down, norm_final):
    pos = jnp.arange(x.shape[1])
    for l in range(DEPTH):
        x = x + parallel_mixer(x, pos, norm_mix[l], w_in[l], w_gla_decay[l], b_gla_decay[l], gla_out_norm[l],
                               swa_sink[l], b_merge[l], w_proj_gla[l], w_proj_swa[l], w_out[l])
        x = x + cross_attention(x, mem, norm_cross[l], norm_mem[l], w_cross_q[l], w_cross_kv[l], w_cross_o[l])
        x = x + moe(x, norm_moe[l], w_router[l], b_router[l], w_up[l], b_up[l], w_down[l], b_down[l])
    return rms_norm(x, norm_final)


def setup_inputs(seed: int = 0) -> dict:
    key = jax.random.key(seed)
    ks = jax.random.split(key, 32)
    f32 = jnp.float32
    nrm = lambda k, shape, fan_in: jax.random.normal(k, shape, f32) * (fan_in ** -0.5)
    gain = lambda k, shape: 1.0 + 0.02 * jax.random.normal(k, shape, f32)
    small = lambda k, shape, s: s * jax.random.normal(k, shape, f32)
    L, D = DEPTH, D_MODEL
    return {
        "x_prompt": jax.random.normal(ks[0], (BATCH, SEQ, D), f32),
        "x_sample": jax.random.normal(ks[1], (DEC_BATCH, DEC_SEQ, D), f32),
        "mem_prompt": jax.random.normal(ks[2], (BATCH, N_MEM, D), f32),
        "mem_sample": jax.random.normal(ks[3], (DEC_BATCH, N_MEM, D), f32),
        "norm_mix": gain(ks[4], (L, D)),
        "w_in": nrm(ks[5], (L, D, IN_COLS), D),
        "w_gla_decay": nrm(ks[6], (L, 2, GLA_GATE_RANK, GLA_QK_W), GLA_GATE_RANK),
        "b_gla_decay": small(ks[7], (L, 2, GLA_QK_W), 0.02),
        "gla_out_norm": gain(ks[8], (L, GLA_V_W)),
        "swa_sink": small(ks[9], (L, SWA_HEADS), 1.0),
        "b_merge": small(ks[10], (L, 2, D), 0.02),
        "w_proj_gla": nrm(ks[11], (L, GLA_V_W, D), GLA_V_W),
        "w_proj_swa": nrm(ks[12], (L, SWA_Q_W, D), SWA_Q_W),
        "w_out": nrm(ks[13], (L, D, D), D),
        "norm_cross": gain(ks[14], (L, D)),
        "norm_mem": gain(ks[15], (L, D)),
        "w_cross_q": nrm(ks[16], (L, D, CROSS_W), D),
        "w_cross_kv": nrm(ks[17], (L, D, 2 * CROSS_W), D),
        "w_cross_o": nrm(ks[18], (L, CROSS_W, D), CROSS_W),
        "norm_moe": gain(ks[19], (L, D)),
        "w_router": nrm(ks[20], (L, D, N_EXPERTS), D),
        "b_router": small(ks[21], (L, N_EXPERTS), 0.01),
        "w_up": nrm(ks[22], (L, N_EXPERTS, D, 2 * D_FF), D),
        "b_up": small(ks[23], (L, N_EXPERTS, 2 * D_FF), 0.02),
        "w_down": nrm(ks[24], (L, N_EXPERTS, D_FF, D), D_FF),
        "b_down": small(ks[25], (L, N_EXPERTS, D), 0.02),
        "norm_final": gain(ks[26], (D,)),
    }


def reference(x_prompt, x_sample, mem_prompt, mem_sample, norm_mix, w_in, w_gla_decay, b_gla_decay, gla_out_norm,
              swa_sink, b_merge, w_proj_gla, w_proj_swa, w_out, norm_cross, norm_mem, w_cross_q, w_cross_kv,
              w_cross_o, norm_moe, w_router, b_router, w_up, b_up, w_down, b_down, norm_final):
    weights = (norm_mix, w_in, w_gla_decay, b_gla_decay, gla_out_norm, swa_sink, b_merge, w_proj_gla, w_proj_swa,
               w_out, norm_cross, norm_mem, w_cross_q, w_cross_kv, w_cross_o, norm_moe, w_router, b_router,
               w_up, b_up, w_down, b_down, norm_final)
    y_prompt = encoder_trunk(x_prompt, mem_prompt, *weights)
    y_sample = encoder_trunk(x_sample, mem_sample, *weights)
    return (y_prompt, y_sample)
```

```python
import functools

import jax
import jax.numpy as jnp
from jax import lax
from jax.experimental import pallas as pl
from jax.experimental.pallas import tpu as pltpu

F32 = jnp.float32
BF16 = jnp.bfloat16

D_MODEL = 2048
EPS = 1e-5
N_MEM = 256

GLA_HEADS = 4
GLA_DK = 128
GLA_DV = 256
GLA_QK_W = GLA_HEADS * GLA_DK
GLA_V_W = GLA_HEADS * GLA_DV
GLA_GATE_RANK = 16
GLA_GATE_NORM = 16.0
GLA_CHUNK = 64

SWA_HEADS = 8
SWA_KV_HEADS = 4
SWA_HEAD_DIM = 128
SWA_WINDOW = 128
SWA_Q_W = SWA_HEADS * SWA_HEAD_DIM
SWA_KV_W = SWA_KV_HEADS * SWA_HEAD_DIM
ROPE_THETA = 500000.0
ROPE_DIM = SWA_HEAD_DIM // 4
ROPE_HALF = ROPE_DIM // 2

CROSS_HEADS = 4
CROSS_HEAD_DIM = 128
CROSS_W = CROSS_HEADS * CROSS_HEAD_DIM

N_EXPERTS = 32
TOP_K = 4
D_FF = D_MODEL // 8
SWIGLU_LIMIT = 7.0
SWIGLU_ALPHA = 1.702

LANES = 128

COL_GATES = 0
COL_GQ = 2 * D_MODEL
COL_GK = COL_GQ + GLA_QK_W
COL_GV = COL_GK + GLA_QK_W
COL_GR = COL_GV + GLA_V_W
COL_SQ = COL_GR + GLA_V_W
COL_SK = COL_SQ + SWA_Q_W
COL_SV = COL_SK + SWA_KV_W
PROJ_COLS = COL_SV + SWA_KV_W
PROJ_TN = 1024
J_SQ = COL_SQ // PROJ_TN
J_SKV = COL_SK // PROJ_TN

MIB = 1024 * 1024


def _cparams(semantics, vmem_mib):
    return pltpu.CompilerParams(dimension_semantics=semantics, vmem_limit_bytes=vmem_mib * MIB)


def _rms(xf, g):
    y = xf * lax.rsqrt(jnp.mean(xf * xf, axis=-1, keepdims=True) + EPS)
    return y * g


def _rope(a, cos_ref, sin_ref):
    width = a.shape[1]
    nh = width // SWA_HEAD_DIM
    cos = jnp.concatenate([cos_ref[...]] * nh, axis=1)
    sin = jnp.concatenate([sin_ref[...]] * nh, axis=1)
    lane = lax.broadcasted_iota(jnp.int32, a.shape, 1) & (SWA_HEAD_DIM - 1)
    partner = jnp.where(lane < ROPE_HALF, pltpu.roll(a, width - ROPE_HALF, 1), pltpu.roll(a, ROPE_HALF, 1))
    return a * cos + partner * sin


def _inproj_kernel(x_ref, g_ref, w_ref, wlr_ref, cos_ref, sin_ref, proj_ref, lr_ref, h_ref):
    j = pl.program_id(1)

    @pl.when(j == 0)
    def _():
        h = _rms(x_ref[...], g_ref[...]).astype(BF16)
        h_ref[...] = h
        lr_ref[...] = jnp.dot(h, wlr_ref[...], preferred_element_type=F32)

    acc = jnp.dot(h_ref[...], w_ref[...], preferred_element_type=F32)

    @pl.when(j == J_SQ)
    def _():
        proj_ref[...] = _rope(acc, cos_ref, sin_ref).astype(BF16)

    @pl.when(j == J_SKV)
    def _():
        roped = _rope(acc[:, :SWA_KV_W], cos_ref, sin_ref)
        proj_ref[...] = jnp.concatenate([roped, acc[:, SWA_KV_W:]], axis=1).astype(BF16)

    @pl.when(jnp.logical_and(j != J_SQ, j != J_SKV))
    def _():
        proj_ref[...] = acc.astype(BF16)


def _in_projection(x2d, seq, g, w_main, w_lr, cos_t, sin_t, tm):
    t = x2d.shape[0]
    pos_blocks = seq // tm
    return pl.pallas_call(
        _inproj_kernel,
        grid=(t // tm, PROJ_COLS // PROJ_TN),
        in_specs=[
            pl.BlockSpec((tm, D_MODEL), lambda i, j: (i, 0)),
            pl.BlockSpec((1, D_MODEL), lambda i, j: (0, 0)),
            pl.BlockSpec((D_MODEL, PROJ_TN), lambda i, j: (0, j)),
            pl.BlockSpec((D_MODEL, LANES), lambda i, j: (0, 0)),
            pl.BlockSpec((tm, SWA_HEAD_DIM), lambda i, j: (i % pos_blocks, 0)),
            pl.BlockSpec((tm, SWA_HEAD_DIM), lambda i, j: (i % pos_blocks, 0)),
        ],
        out_specs=[
            pl.BlockSpec((tm, PROJ_TN), lambda i, j: (i, j)),
            pl.BlockSpec((tm, LANES), lambda i, j: (i, 0)),
        ],
        out_shape=[
            jax.ShapeDtypeStruct((t, PROJ_COLS), BF16),
            jax.ShapeDtypeStruct((t, LANES), F32),
        ],
        scratch_shapes=[pltpu.VMEM((tm, D_MODEL), BF16)],
        compiler_params=_cparams(("parallel", "arbitrary"), 52),
        name="in_projection",
    )(x2d, g, w_main, w_lr, cos_t, sin_t)


def _gla_kernel(q_ref, k_ref, v_ref, r_ref, lr_ref, wdec_ref, bdec_ref, gout_ref, o_ref,
                oacc_ref, sf_ref, sb_ref):
    seq = q_ref.shape[1]
    c = GLA_CHUNK
    n_chunks = seq // c
    half = n_chunks // 2
    scale = GLA_DK ** -0.5

    sf_ref[...] = jnp.zeros_like(sf_ref)
    sb_ref[...] = jnp.zeros_like(sb_ref)

    row = lax.broadcasted_iota(jnp.int32, (c, c), 0)
    col = lax.broadcasted_iota(jnp.int32, (c, c), 1)
    cum_mat = (jnp.where(col <= row, 1.0, 0.0).astype(BF16), jnp.where(col >= row, 1.0, 0.0).astype(BF16))
    att_mask = (col <= row, col > row)
    state_refs = (sf_ref, sb_ref)

    def chunk_out(n, d):
        rows = pl.ds(pl.multiple_of(n * c, c), c)
        q = q_ref[0, rows, :].astype(F32) * scale
        k = k_ref[0, rows, :].astype(F32)
        v = v_ref[0, rows, :]
        z = jnp.dot(lr_ref[0, rows, :].astype(BF16), wdec_ref[d], preferred_element_type=F32) + bdec_ref[d]
        log_a = (jnp.minimum(z, 0.0) - jnp.log(1.0 + jnp.exp(-jnp.abs(z)))) / GLA_GATE_NORM
        la_hi = log_a.astype(BF16)
        la_lo = (log_a - la_hi.astype(F32)).astype(BF16)
        b = (jnp.dot(cum_mat[d], la_hi, preferred_element_type=F32)
             + jnp.dot(cum_mat[d], la_lo, preferred_element_type=F32))
        b_edge = b[c - 1:c, :] if d == 0 else b[0:1, :]
        q_dec = (q * jnp.exp(b)).astype(BF16)
        k_dec = (k * jnp.exp(-b)).astype(BF16)
        k_end = (k * jnp.exp(b_edge - b)).astype(BF16)
        att = lax.dot_general(q_dec, k_dec, (((1,), (1,)), ((), ())), preferred_element_type=F32)
        att = jnp.where(att_mask[d], att, 0.0).astype(BF16)
        s_ref = state_refs[d]
        state = s_ref[...]
        o = jnp.dot(att, v, preferred_element_type=F32)
        o = o + lax.dot_general(q_dec, state.astype(BF16), (((1,), (1,)), ((), ())),
                                preferred_element_type=F32)
        kv = lax.dot_general(v, k_end, (((0,), (0,)), ((), ())), preferred_element_type=F32)
        s_ref[...] = jnp.exp(b_edge) * state + kv
        return rows, o

    def finalize(rows, o):
        o = o * lax.rsqrt(jnp.mean(o * o, axis=-1, keepdims=True) + EPS) * gout_ref[...]
        r = r_ref[0, rows, :].astype(F32)
        o_ref[0, rows, :] = (o * (r * jax.nn.sigmoid(r))).astype(BF16)

    def first_half(i, carry):
        for d, n in ((0, i), (1, n_chunks - 1 - i)):
            rows, o = chunk_out(n, d)
            oacc_ref[rows, :] = o
        return carry

    def second_half(i, carry):
        for d, n in ((0, i), (1, n_chunks - 1 - i)):
            rows, o = chunk_out(n, d)
            finalize(rows, oacc_ref[rows, :] + o)
        return carry

    lax.fori_loop(0, half, first_half, 0)
    lax.fori_loop(half, n_chunks, second_half, 0)


def _gla(proj3, lr3, wdec_pad, bdec, gout):
    bsz, seq, _ = proj3.shape
    qb, kb = COL_GQ // GLA_DK, COL_GK // GLA_DK
    vb, rb = COL_GV // GLA_DV, COL_GR // GLA_DV
    return pl.pallas_call(
        _gla_kernel,
        grid=(bsz, GLA_HEADS),
        in_specs=[
            pl.BlockSpec((1, seq, GLA_DK), lambda b, h: (b, 0, qb + h)),
            pl.BlockSpec((1, seq, GLA_DK), lambda b, h: (b, 0, kb + h)),
            pl.BlockSpec((1, seq, GLA_DV), lambda b, h: (b, 0, vb + h)),
            pl.BlockSpec((1, seq, GLA_DV), lambda b, h: (b, 0, rb + h)),
            pl.BlockSpec((1, seq, LANES), lambda b, h: (b, 0, 0)),
            pl.BlockSpec((2, LANES, GLA_DK), lambda b, h: (0, 0, h)),
            pl.BlockSpec((2, 1, GLA_DK), lambda b, h: (0, 0, h)),
            pl.BlockSpec((1, GLA_DV), lambda b, h: (0, h)),
        ],
        out_specs=pl.BlockSpec((1, seq, GLA_DV), lambda b, h: (b, 0, h)),
        out_shape=jax.ShapeDtypeStruct((bsz, seq, GLA_V_W), BF16),
        scratch_shapes=[
            pltpu.VMEM((seq, GLA_DV), F32),
            pltpu.VMEM((GLA_DV, GLA_DK), F32),
            pltpu.VMEM((GLA_DV, GLA_DK), F32),
        ],
        compiler_params=_cparams(("parallel", "parallel"), 48),
        name="gla",
    )(proj3, proj3, proj3, proj3, lr3, wdec_pad, bdec, gout)


def _swa_kernel(sink_ref, q_ref, kp_ref, kc_ref, kn_ref, vp_ref, vc_ref, vn_ref, o_ref, kbuf_ref, vbuf_ref):
    t = pl.program_id(1)
    tq = q_ref.shape[1]
    w = SWA_WINDOW
    hd = SWA_HEAD_DIM
    groups = SWA_HEADS // SWA_KV_HEADS
    seq = pl.num_programs(1) * tq
    scale = hd ** -0.5

    kbuf_ref[0:w, :] = kp_ref[0]
    kbuf_ref[w:w + tq, :] = kc_ref[0]
    kbuf_ref[w + tq:, :] = kn_ref[0]
    vbuf_ref[0:w, :] = vp_ref[0]
    vbuf_ref[w:w + tq, :] = vc_ref[0]
    vbuf_ref[w + tq:, :] = vn_ref[0]

    qi = lax.broadcasted_iota(jnp.int32, (groups * w, 3 * w), 0) & (w - 1)
    kj = lax.broadcasted_iota(jnp.int32, (groups * w, 3 * w), 1)
    in_window = jnp.abs(kj - w - qi) <= w

    def block(m, carry):
        q0 = pl.multiple_of(m * w, w)
        kpos = t * tq + q0 - w + kj
        valid = in_window & (kpos >= 0) & (kpos < seq)
        for kh in range(SWA_KV_HEADS):
            qs = jnp.concatenate(
                [q_ref[0, pl.ds(q0, w), (kh * groups + g) * hd:(kh * groups + g + 1) * hd] for g in range(groups)],
                axis=0)
            ks = kbuf_ref[pl.ds(q0, 3 * w), kh * hd:(kh + 1) * hd]
            vs = vbuf_ref[pl.ds(q0, 3 * w), kh * hd:(kh + 1) * hd]
            s = lax.dot_general(qs, ks, (((1,), (1,)), ((), ())), preferred_element_type=F32) * scale
            s = jnp.where(valid, s, -jnp.inf)
            row_g = lax.broadcasted_iota(jnp.int32, (groups * w, 1), 0) // w
            sink = jnp.zeros((groups * w, 1), F32)
            for g in range(groups):
                sink = jnp.where(row_g == g, sink_ref[kh * groups + g], sink)
            mx = jnp.maximum(jnp.max(s, axis=-1, keepdims=True), sink)
            p = jnp.exp(s - mx)
            denom = jnp.sum(p, axis=-1, keepdims=True) + jnp.exp(sink - mx)
            o = jnp.dot(p.astype(BF16), vs, preferred_element_type=F32) / denom
            for g in range(groups):
                head = kh * groups + g
                o_ref[0, pl.ds(q0, w), head * hd:(head + 1) * hd] = o[g * w:(g + 1) * w, :].astype(BF16)
        return carry

    lax.fori_loop(0, tq // w, block, 0)


def _swa(proj3, sink, tq):
    bsz, seq, _ = proj3.shape
    w = SWA_WINDOW
    r = tq // w
    last = seq // w - 1
    qb = COL_SQ // SWA_Q_W
    kb, vb = COL_SK // SWA_KV_W, COL_SV // SWA_KV_W

    def neighbours(cb):
        return [
            pl.BlockSpec((1, w, SWA_KV_W), lambda b, t: (b, jnp.maximum(t * r - 1, 0), cb)),
            pl.BlockSpec((1, tq, SWA_KV_W), lambda b, t: (b, t, cb)),
            pl.BlockSpec((1, w, SWA_KV_W), lambda b, t: (b, jnp.minimum((t + 1) * r, last), cb)),
        ]

    return pl.pallas_call(
        _swa_kernel,
        grid=(bsz, seq // tq),
        in_specs=[pl.BlockSpec(memory_space=pltpu.SMEM),
                  pl.BlockSpec((1, tq, SWA_Q_W), lambda b, t: (b, t, qb))] + neighbours(kb) + neighbours(vb),
        out_specs=pl.BlockSpec((1, tq, SWA_Q_W), lambda b, t: (b, t, 0)),
        out_shape=jax.ShapeDtypeStruct((bsz, seq, SWA_Q_W), BF16),
        scratch_shapes=[pltpu.VMEM((tq + 2 * w, SWA_KV_W), BF16), pltpu.VMEM((tq + 2 * w, SWA_KV_W), BF16)],
        compiler_params=_cparams(("parallel", "parallel"), 32),
        name="swa",
    )(sink, proj3, proj3, proj3, proj3, proj3, proj3, proj3)


def _merge_kernel(x_ref, og_ref, os_ref, g0_ref, g1_ref, bm_ref, wg_ref, ws_ref, wo_ref, o_ref):
    a = jnp.dot(og_ref[...], wg_ref[...], preferred_element_type=F32)
    b = jnp.dot(os_ref[...], ws_ref[...], preferred_element_type=F32)
    g0 = jax.nn.sigmoid(g0_ref[...].astype(F32) + bm_ref[0:1, :])
    g1 = jax.nn.sigmoid(g1_ref[...].astype(F32) + bm_ref[1:2, :])
    merged = (g0 * a + g1 * b).astype(BF16)
    o_ref[...] = x_ref[...] + jnp.dot(merged, wo_ref[...], preferred_element_type=F32)


def _resident(shape):
    return pl.BlockSpec(shape, lambda *_: (0,) * len(shape), pipeline_mode=pl.Buffered(1))


def _merge(x2d, o_gla, o_swa, proj, b_merge, w_pg, w_ps, w_out, tm):
    t = x2d.shape[0]
    return pl.pallas_call(
        _merge_kernel,
        grid=(t // tm,),
        in_specs=[
            pl.BlockSpec((tm, D_MODEL), lambda i: (i, 0)),
            pl.BlockSpec((tm, GLA_V_W), lambda i: (i, 0)),
            pl.BlockSpec((tm, SWA_Q_W), lambda i: (i, 0)),
            pl.BlockSpec((tm, D_MODEL), lambda i: (i, 0)),
            pl.BlockSpec((tm, D_MODEL), lambda i: (i, 1)),
            _resident((2, D_MODEL)),
            _resident((GLA_V_W, D_MODEL)),
            _resident((SWA_Q_W, D_MODEL)),
            _resident((D_MODEL, D_MODEL)),
        ],
        out_specs=pl.BlockSpec((tm, D_MODEL), lambda i: (i, 0)),
        out_shape=jax.ShapeDtypeStruct((t, D_MODEL), F32),
        compiler_params=_cparams(("parallel",), 56),
        name="merge",
    )(x2d, o_gla, o_swa, proj, proj, b_merge, w_pg, w_ps, w_out)


def _norm_matmul_kernel(x_ref, g_ref, w_ref, o_ref):
    h = _rms(x_ref[...], g_ref[...]).astype(BF16)
    o_ref[...] = jnp.dot(h, w_ref[...], preferred_element_type=F32).astype(o_ref.dtype)


def _mem_kv(mem2d, g, w_kv, tm):
    t = mem2d.shape[0]
    n = w_kv.shape[1]
    return pl.pallas_call(
        _norm_matmul_kernel,
        grid=(t // tm,),
        in_specs=[
            pl.BlockSpec((tm, D_MODEL), lambda i: (i, 0)),
            _resident((1, D_MODEL)),
            _resident((D_MODEL, n)),
        ],
        out_specs=pl.BlockSpec((tm, n), lambda i: (i, 0)),
        out_shape=jax.ShapeDtypeStruct((t, n), BF16),
        compiler_params=_cparams(("parallel",), 32),
        name="mem_kv",
    )(mem2d, g, w_kv)


def _cross_kernel(x_ref, g_ref, kv_ref, wq_ref, wo_ref, o_ref):
    x = x_ref[0]
    h = _rms(x, g_ref[...]).astype(BF16)
    q = jnp.dot(h, wq_ref[...], preferred_element_type=F32).astype(BF16)
    hd = CROSS_HEAD_DIM
    scale = hd ** -0.5
    outs = []
    for head in range(CROSS_HEADS):
        qh = q[:, head * hd:(head + 1) * hd]
        kh = kv_ref[0, :, head * hd:(head + 1) * hd]
        vh = kv_ref[0, :, CROSS_W + head * hd:CROSS_W + (head + 1) * hd]
        s = lax.dot_general(qh, kh, (((1,), (1,)), ((), ())), preferred_element_type=F32) * scale
        p = jnp.exp(s - jnp.max(s, axis=-1, keepdims=True))
        denom = jnp.sum(p, axis=-1, keepdims=True)
        outs.append((jnp.dot(p.astype(BF16), vh, preferred_element_type=F32) / denom).astype(BF16))
    o = jnp.concatenate(outs, axis=1)
    o_ref[0] = x + jnp.dot(o, wo_ref[...], preferred_element_type=F32)


def _cross(x3, kv3, g, w_q, w_o, tm):
    bsz, seq, _ = x3.shape
    return pl.pallas_call(
        _cross_kernel,
        grid=(bsz, seq // tm),
        in_specs=[
            pl.BlockSpec((1, tm, D_MODEL), lambda b, i: (b, i, 0)),
            _resident((1, D_MODEL)),
            pl.BlockSpec((1, N_MEM, 2 * CROSS_W), lambda b, i: (b, 0, 0)),
            _resident((D_MODEL, CROSS_W)),
            _resident((CROSS_W, D_MODEL)),
        ],
        out_specs=pl.BlockSpec((1, tm, D_MODEL), lambda b, i: (b, i, 0)),
        out_shape=jax.ShapeDtypeStruct((bsz, seq, D_MODEL), F32),
        compiler_params=_cparams(("parallel", "parallel"), 48),
        name="cross_attention",
    )(x3, g, kv3, w_q, w_o)


def _moe_kernel(x_ref, g_ref, wr_ref, br_ref, wup_ref, bup_ref, wdn_ref, bdn_ref, gf_ref, o_ref,
                h_ref, comb_ref):
    e = pl.program_id(1)
    lane = lax.broadcasted_iota(jnp.int32, comb_ref.shape, 1)

    @pl.when(e == 0)
    def _():
        hf = _rms(x_ref[...], g_ref[...])
        h_hi = hf.astype(BF16)
        h_ref[...] = h_hi
        h_lo = (hf - h_hi.astype(F32)).astype(BF16)
        logits = (jnp.dot(h_hi, wr_ref[0], preferred_element_type=F32)
                  + jnp.dot(h_lo, wr_ref[0], preferred_element_type=F32)
                  + jnp.dot(h_hi, wr_ref[1], preferred_element_type=F32)) + br_ref[...]
        logits = jnp.where(lane < N_EXPERTS, logits, -jnp.inf)
        vals, hots = [], []
        for _ in range(TOP_K):
            mx = jnp.max(logits, axis=-1, keepdims=True)
            idx = jnp.min(jnp.where(logits == mx, lane, LANES), axis=-1, keepdims=True)
            hot = lane == idx
            vals.append(mx)
            hots.append(hot)
            logits = jnp.where(hot, -jnp.inf, logits)
        exps = [jnp.exp(v - vals[0]) for v in vals]
        denom = exps[0] + exps[1] + exps[2] + exps[3]
        comb = jnp.zeros(comb_ref.shape, F32)
        for ex, hot in zip(exps, hots):
            comb = comb + jnp.where(hot, ex / denom, 0.0)
        comb_ref[...] = comb
        o_ref[...] = jnp.dot(comb.astype(BF16), bdn_ref[...], preferred_element_type=F32)

    up = jnp.dot(h_ref[...], wup_ref[0], preferred_element_type=F32) + bup_ref[0]
    glu = jnp.minimum(up[:, :D_FF], SWIGLU_LIMIT)
    lin = jnp.clip(up[:, D_FF:], -SWIGLU_LIMIT, SWIGLU_LIMIT)
    gate = jnp.sum(jnp.where(lane == e, comb_ref[...], 0.0), axis=-1, keepdims=True)
    act = (glu * jax.nn.sigmoid(SWIGLU_ALPHA * glu) * (lin + 1.0)) * gate
    o_ref[...] += jnp.dot(act.astype(BF16), wdn_ref[0], preferred_element_type=F32)

    @pl.when(e == N_EXPERTS - 1)
    def _():
        o_ref[...] = _rms(x_ref[...] + o_ref[...], gf_ref[...])


def _moe(x2d, g, w_router, b_router, w_up, b_up, w_down, b_down, g_final, tm):
    t = x2d.shape[0]
    return pl.pallas_call(
        _moe_kernel,
        grid=(t // tm, N_EXPERTS),
        in_specs=[
            pl.BlockSpec((tm, D_MODEL), lambda i, e: (i, 0)),
            pl.BlockSpec((1, D_MODEL), lambda i, e: (0, 0)),
            pl.BlockSpec((2, D_MODEL, LANES), lambda i, e: (0, 0, 0)),
            pl.BlockSpec((1, LANES), lambda i, e: (0, 0)),
            pl.BlockSpec((1, D_MODEL, 2 * D_FF), lambda i, e: (e, 0, 0)),
            pl.BlockSpec((1, 1, 2 * D_FF), lambda i, e: (e, 0, 0)),
            pl.BlockSpec((1, D_FF, D_MODEL), lambda i, e: (e, 0, 0)),
            pl.BlockSpec((LANES, D_MODEL), lambda i, e: (0, 0)),
            pl.BlockSpec((1, D_MODEL), lambda i, e: (0, 0)),
        ],
        out_specs=pl.BlockSpec((tm, D_MODEL), lambda i, e: (i, 0)),
        out_shape=jax.ShapeDtypeStruct((t, D_MODEL), F32),
        scratch_shapes=[pltpu.VMEM((tm, D_MODEL), BF16), pltpu.VMEM((tm, LANES), F32)],
        compiler_params=_cparams(("parallel", "arbitrary"), 48),
        name="moe",
    )(x2d, g, w_router, b_router, w_up, b_up, w_down, b_down, g_final)


def _rope_tables(seq):
    inv = ROPE_THETA ** (-jnp.arange(ROPE_HALF, dtype=F32) / ROPE_HALF)
    ang = jnp.arange(seq, dtype=F32)[:, None] * inv[None, :]
    cos, sin = jnp.cos(ang), jnp.sin(ang)
    pad = SWA_HEAD_DIM - ROPE_DIM
    cos_t = jnp.concatenate([cos, cos, jnp.ones((seq, pad), F32)], axis=1)
    sin_t = jnp.concatenate([-sin, sin, jnp.zeros((seq, pad), F32)], axis=1)
    return cos_t, sin_t


def _prepare(norm_mix, w_in, w_gla_decay, b_gla_decay, gla_out_norm, swa_sink, b_merge, w_proj_gla, w_proj_swa,
             w_out, norm_cross, norm_mem, w_cross_q, w_cross_kv, w_cross_o, norm_moe, w_router, b_router,
             w_up, b_up, w_down, b_down, norm_final):
    p = {}
    lr0 = 2 * GLA_QK_W + 2 * GLA_V_W
    lr1 = lr0 + 2 * GLA_GATE_RANK
    g0 = lr1 + SWA_Q_W + 2 * SWA_KV_W
    p["w_main"] = jnp.concatenate([w_in[:, g0:], w_in[:, :lr0], w_in[:, lr1:g0]], axis=1).astype(BF16)
    p["w_lr"] = jnp.pad(w_in[:, lr0:lr1], ((0, 0), (0, LANES - 2 * GLA_GATE_RANK))).astype(BF16)
    wdec = jnp.zeros((2, LANES, GLA_QK_W), F32)
    for d in range(2):
        wdec = wdec.at[d, d * GLA_GATE_RANK:(d + 1) * GLA_GATE_RANK, :].set(w_gla_decay[d])
    p["w_dec"] = wdec.astype(BF16)
    p["b_dec"] = b_gla_decay.reshape(2, 1, GLA_QK_W)
    p["g_gla"] = gla_out_norm.reshape(1, GLA_V_W)
    p["sink"] = swa_sink
    p["norm_mix"] = norm_mix.reshape(1, D_MODEL)
    p["b_merge"] = b_merge
    p["w_pg"] = w_proj_gla.astype(BF16)
    p["w_ps"] = w_proj_swa.astype(BF16)
    p["w_out"] = w_out.astype(BF16)
    p["norm_cross"] = norm_cross.reshape(1, D_MODEL)
    p["norm_mem"] = norm_mem.reshape(1, D_MODEL)
    p["w_cq"] = w_cross_q.astype(BF16)
    p["w_ckv"] = w_cross_kv.astype(BF16)
    p["w_co"] = w_cross_o.astype(BF16)
    p["norm_moe"] = norm_moe.reshape(1, D_MODEL)
    wr = jnp.pad(w_router, ((0, 0), (0, LANES - N_EXPERTS)))
    wr_hi = wr.astype(BF16)
    p["w_router"] = jnp.stack([wr_hi, (wr - wr_hi.astype(F32)).astype(BF16)])
    p["b_router"] = jnp.pad(b_router, (0, LANES - N_EXPERTS)).reshape(1, LANES)
    p["w_up"] = jnp.concatenate([w_up[:, :, 0::2], w_up[:, :, 1::2]], axis=2).astype(BF16)
    p["b_up"] = jnp.concatenate([b_up[:, 0::2], b_up[:, 1::2]], axis=1).reshape(N_EXPERTS, 1, 2 * D_FF)
    p["w_down"] = w_down.astype(BF16)
    p["b_down"] = jnp.pad(b_down, ((0, LANES - N_EXPERTS), (0, 0))).astype(BF16)
    p["norm_final"] = norm_final.reshape(1, D_MODEL)
    return p


def _trunk(x, mem, p):
    bsz, seq, _ = x.shape
    t = bsz * seq
    x2d = x.reshape(t, D_MODEL)
    cos_t, sin_t = _rope_tables(seq)
    tm_in = min(1024, seq)
    proj, lr = _in_projection(x2d, seq, p["norm_mix"], p["w_main"], p["w_lr"], cos_t, sin_t, tm_in)
    proj3 = proj.reshape(bsz, seq, PROJ_COLS)
    o_gla = _gla(proj3, lr.reshape(bsz, seq, LANES), p["w_dec"], p["b_dec"], p["g_gla"])
    o_swa = _swa(proj3, p["sink"], min(512, seq))
    x1 = _merge(x2d, o_gla.reshape(t, GLA_V_W), o_swa.reshape(t, SWA_Q_W), proj, p["b_merge"],
                p["w_pg"], p["w_ps"], p["w_out"], min(512, seq))
    kv = _mem_kv(mem.reshape(bsz * N_MEM, D_MODEL), p["norm_mem"], p["w_ckv"], N_MEM)
    x2 = _cross(x1.reshape(bsz, seq, D_MODEL), kv.reshape(bsz, N_MEM, 2 * CROSS_W), p["norm_cross"],
                p["w_cq"], p["w_co"], min(512, seq))
    y = _moe(x2.reshape(t, D_MODEL), p["norm_moe"], p["w_router"], p["b_router"], p["w_up"], p["b_up"],
             p["w_down"], p["b_down"], p["norm_final"], min(512, seq))
    return y.reshape(bsz, seq, D_MODEL)


def kernel(x_prompt, x_sample, mem_prompt, mem_sample, norm_mix, w_in, w_gla_decay, b_gla_decay, gla_out_norm,
           swa_sink, b_merge, w_proj_gla, w_proj_swa, w_out, norm_cross, norm_mem, w_cross_q, w_cross_kv,
           w_cross_o, norm_moe, w_router, b_router, w_up, b_up, w_down, b_down, norm_final):
    assert norm_mix.shape[0] == 1, "single-layer stack"
    p = _prepare(norm_mix[0], w_in[0], w_gla_decay[0], b_gla_decay[0], gla_out_norm[0], swa_sink[0], b_merge[0],
                 w_proj_gla[0], w_proj_swa[0], w_out[0], norm_cross[0], norm_mem[0], w_cross_q[0],
                 w_cross_kv[0], w_cross_o[0], norm_moe[0], w_router[0], b_router[0], w_up[0], b_up[0],
                 w_down[0], b_down[0], norm_final)
    return (_trunk(x_prompt, mem_prompt, p), _trunk(x_sample, mem_sample, p))
```

```python
import functools

import jax
import jax.numpy as jnp
from jax import lax
from jax.experimental import pallas as pl
from jax.experimental.pallas import tpu as pltpu

F32 = jnp.float32
BF16 = jnp.bfloat16

D_MODEL = 2048
EPS = 1e-5
N_MEM = 256

GLA_HEADS = 4
GLA_DK = 128
GLA_DV = 256
GLA_QK_W = GLA_HEADS * GLA_DK
GLA_V_W = GLA_HEADS * GLA_DV
GLA_GATE_RANK = 16
GLA_GATE_NORM = 16.0
GLA_CHUNK = 64

SWA_HEADS = 8
SWA_KV_HEADS = 4
SWA_HEAD_DIM = 128
SWA_WINDOW = 128
SWA_Q_W = SWA_HEADS * SWA_HEAD_DIM
SWA_KV_W = SWA_KV_HEADS * SWA_HEAD_DIM
ROPE_THETA = 500000.0
ROPE_DIM = SWA_HEAD_DIM // 4
ROPE_HALF = ROPE_DIM // 2

CROSS_HEADS = 4
CROSS_HEAD_DIM = 128
CROSS_W = CROSS_HEADS * CROSS_HEAD_DIM

N_EXPERTS = 32
TOP_K = 4
D_FF = D_MODEL // 8
SWIGLU_LIMIT = 7.0
SWIGLU_ALPHA = 1.702

LANES = 128

COL_GATES = 0
COL_GQ = 2 * D_MODEL
COL_GK = COL_GQ + GLA_QK_W
COL_GV = COL_GK + GLA_QK_W
COL_GR = COL_GV + GLA_V_W
COL_SQ = COL_GR + GLA_V_W
COL_SK = COL_SQ + SWA_Q_W
COL_SV = COL_SK + SWA_KV_W
PROJ_COLS = COL_SV + SWA_KV_W
PROJ_TN = 1024
J_SQ = COL_SQ // PROJ_TN
J_SKV = COL_SK // PROJ_TN

MIB = 1024 * 1024


def _cparams(semantics, vmem_mib):
    return pltpu.CompilerParams(dimension_semantics=semantics, vmem_limit_bytes=vmem_mib * MIB)


def _rms(xf, g):
    y = xf * lax.rsqrt(jnp.mean(xf * xf, axis=-1, keepdims=True) + EPS)
    return y * g


def _rope(a, cos_ref, sin_ref):
    width = a.shape[1]
    nh = width // SWA_HEAD_DIM
    cos = jnp.concatenate([cos_ref[...]] * nh, axis=1)
    sin = jnp.concatenate([sin_ref[...]] * nh, axis=1)
    lane = lax.broadcasted_iota(jnp.int32, a.shape, 1) & (SWA_HEAD_DIM - 1)
    partner = jnp.where(lane < ROPE_HALF, pltpu.roll(a, width - ROPE_HALF, 1), pltpu.roll(a, ROPE_HALF, 1))
    return a * cos + partner * sin


def _inproj_kernel(x_ref, g_ref, w_ref, wlr_ref, cos_ref, sin_ref, proj_ref, lr_ref, h_ref):
    j = pl.program_id(1)

    @pl.when(j == 0)
    def _():
        h = _rms(x_ref[...], g_ref[...]).astype(BF16)
        h_ref[...] = h
        lr_ref[...] = jnp.dot(h, wlr_ref[...], preferred_element_type=F32).astype(BF16)

    acc = jnp.dot(h_ref[...], w_ref[...], preferred_element_type=F32)

    @pl.when(j == J_SQ)
    def _():
        proj_ref[...] = _rope(acc, cos_ref, sin_ref).astype(BF16)

    @pl.when(j == J_SKV)
    def _():
        roped = _rope(acc[:, :SWA_KV_W], cos_ref, sin_ref)
        proj_ref[...] = jnp.concatenate([roped, acc[:, SWA_KV_W:]], axis=1).astype(BF16)

    @pl.when(jnp.logical_and(j != J_SQ, j != J_SKV))
    def _():
        proj_ref[...] = acc.astype(BF16)


def _in_projection(x2d, seq, g, w_main, w_lr, cos_t, sin_t, tm):
    t = x2d.shape[0]
    pos_blocks = seq // tm
    return pl.pallas_call(
        _inproj_kernel,
        grid=(t // tm, PROJ_COLS // PROJ_TN),
        in_specs=[
            pl.BlockSpec((tm, D_MODEL), lambda i, j: (i, 0)),
            pl.BlockSpec((1, D_MODEL), lambda i, j: (0, 0)),
            pl.BlockSpec((D_MODEL, PROJ_TN), lambda i, j: (0, j)),
            pl.BlockSpec((D_MODEL, LANES), lambda i, j: (0, 0)),
            pl.BlockSpec((tm, SWA_HEAD_DIM), lambda i, j: (i % pos_blocks, 0)),
            pl.BlockSpec((tm, SWA_HEAD_DIM), lambda i, j: (i % pos_blocks, 0)),
        ],
        out_specs=[
            pl.BlockSpec((tm, PROJ_TN), lambda i, j: (i, j)),
            pl.BlockSpec((tm, LANES), lambda i, j: (i, 0)),
        ],
        out_shape=[
            jax.ShapeDtypeStruct((t, PROJ_COLS), BF16),
            jax.ShapeDtypeStruct((t, LANES), BF16),
        ],
        scratch_shapes=[pltpu.VMEM((tm, D_MODEL), BF16)],
        compiler_params=_cparams(("parallel", "arbitrary"), 52),
        name="in_projection",
    )(x2d, g, w_main, w_lr, cos_t, sin_t)


def _gla_kernel(q_ref, k_ref, v_ref, r_ref, lr_ref, wdec_ref, bdec_ref, gout_ref, o_ref,
                oacc_ref, state_ref):
    seq = q_ref.shape[1]
    heads = q_ref.shape[2] // GLA_DK
    c = GLA_CHUNK
    n_chunks = seq // c
    half = n_chunks // 2
    scale = GLA_DK ** -0.5

    state_ref[...] = jnp.zeros_like(state_ref)

    row = lax.broadcasted_iota(jnp.int32, (c, c), 0)
    col = lax.broadcasted_iota(jnp.int32, (c, c), 1)
    cum_mat = (jnp.where(col <= row, 1.0, 0.0).astype(BF16), jnp.where(col >= row, 1.0, 0.0).astype(BF16))
    att_mask = (col <= row, col > row)

    def advance(i):
        rows = [pl.ds(pl.multiple_of(n * c, c), c) for n in (i, n_chunks - 1 - i)]
        zs = [jnp.dot(lr_ref[0, rows[d], :], wdec_ref[d], preferred_element_type=F32) + bdec_ref[d]
              for d in range(2)]
        bs = []
        for d in range(2):
            log_a = (jnp.minimum(zs[d], 0.0) - jnp.log(1.0 + jnp.exp(-jnp.abs(zs[d])))) / GLA_GATE_NORM
            la_hi = log_a.astype(BF16)
            la_lo = (log_a - la_hi.astype(F32)).astype(BF16)
            bs.append(jnp.dot(cum_mat[d], la_hi, preferred_element_type=F32)
                      + jnp.dot(cum_mat[d], la_lo, preferred_element_type=F32))
        work = []
        for d in range(2):
            b = bs[d]
            b_edge = b[c - 1:c, :] if d == 0 else b[0:1, :]
            q = q_ref[0, rows[d], :].astype(F32) * scale
            k = k_ref[0, rows[d], :].astype(F32)
            q_dec = (q * jnp.exp(b)).astype(BF16)
            k_dec = (k * jnp.exp(-b)).astype(BF16)
            k_end = (k * jnp.exp(b_edge - b)).astype(BF16)
            decay = jnp.exp(b_edge)
            for hh in range(heads):
                kc = slice(hh * GLA_DK, (hh + 1) * GLA_DK)
                att = lax.dot_general(q_dec[:, kc], k_dec[:, kc], (((1,), (1,)), ((), ())),
                                      preferred_element_type=F32)
                att = jnp.where(att_mask[d], att, 0.0).astype(BF16)
                work.append((d, hh, q_dec[:, kc], k_end[:, kc], decay[:, kc], att))
        outs = []
        for d, hh, q_dec, k_end, decay, att in work:
            vcols = slice(hh * GLA_DV, (hh + 1) * GLA_DV)
            v = v_ref[0, rows[d], vcols]
            slot = d * heads + hh
            state = state_ref[slot]
            o = jnp.dot(att, v, preferred_element_type=F32)
            o = o + lax.dot_general(q_dec, state.astype(BF16), (((1,), (1,)), ((), ())),
                                    preferred_element_type=F32)
            kv = lax.dot_general(v, k_end, (((0,), (0,)), ((), ())), preferred_element_type=F32)
            state_ref[slot] = decay * state + kv
            outs.append((rows[d], vcols, o))
        return outs

    def finalize(rows, vcols, o):
        o = o * lax.rsqrt(jnp.mean(o * o, axis=-1, keepdims=True) + EPS) * gout_ref[:, vcols]
        r = r_ref[0, rows, vcols].astype(F32)
        o_ref[0, rows, vcols] = (o * (r * jax.nn.sigmoid(r))).astype(BF16)

    def first_half(i, carry):
        for rows, vcols, o in advance(i):
            oacc_ref[rows, vcols] = o
        return carry

    def second_half(i, carry):
        for rows, vcols, o in advance(i):
            finalize(rows, vcols, oacc_ref[rows, vcols] + o)
        return carry

    lax.fori_loop(0, half, first_half, 0)
    lax.fori_loop(half, n_chunks, second_half, 0)


def _gla(proj3, lr3, wdec_pad, bdec, gout, heads_per_step):
    bsz, seq, _ = proj3.shape
    hp = heads_per_step
    kw, vw = hp * GLA_DK, hp * GLA_DV
    qb, kb = COL_GQ // kw, COL_GK // kw
    vb, rb = COL_GV // vw, COL_GR // vw
    return pl.pallas_call(
        _gla_kernel,
        grid=(bsz, GLA_HEADS // hp),
        in_specs=[
            pl.BlockSpec((1, seq, kw), lambda b, h: (b, 0, qb + h)),
            pl.BlockSpec((1, seq, kw), lambda b, h: (b, 0, kb + h)),
            pl.BlockSpec((1, seq, vw), lambda b, h: (b, 0, vb + h)),
            pl.BlockSpec((1, seq, vw), lambda b, h: (b, 0, rb + h)),
            pl.BlockSpec((1, seq, LANES), lambda b, h: (b, 0, 0)),
            pl.BlockSpec((2, LANES, kw), lambda b, h: (0, 0, h)),
            pl.BlockSpec((2, 1, kw), lambda b, h: (0, 0, h)),
            pl.BlockSpec((1, vw), lambda b, h: (0, h)),
        ],
        out_specs=pl.BlockSpec((1, seq, vw), lambda b, h: (b, 0, h)),
        out_shape=jax.ShapeDtypeStruct((bsz, seq, GLA_V_W), BF16),
        scratch_shapes=[
            pltpu.VMEM((seq, vw), F32),
            pltpu.VMEM((2 * hp, GLA_DV, GLA_DK), F32),
        ],
        compiler_params=_cparams(("parallel", "parallel"), 56),
        name="gla",
    )(proj3, proj3, proj3, proj3, lr3, wdec_pad, bdec, gout)


def _swa_kernel(sink_ref, q_ref, kp_ref, kc_ref, kn_ref, vp_ref, vc_ref, vn_ref, o_ref, kbuf_ref, vbuf_ref):
    t = pl.program_id(1)
    tq = q_ref.shape[1]
    w = SWA_WINDOW
    hd = SWA_HEAD_DIM
    groups = SWA_HEADS // SWA_KV_HEADS
    seq = pl.num_programs(1) * tq
    scale = hd ** -0.5

    kbuf_ref[0:w, :] = kp_ref[0]
    kbuf_ref[w:w + tq, :] = kc_ref[0]
    kbuf_ref[w + tq:, :] = kn_ref[0]
    vbuf_ref[0:w, :] = vp_ref[0]
    vbuf_ref[w:w + tq, :] = vc_ref[0]
    vbuf_ref[w + tq:, :] = vn_ref[0]

    qi = lax.broadcasted_iota(jnp.int32, (groups * w, 3 * w), 0) & (w - 1)
    kj = lax.broadcasted_iota(jnp.int32, (groups * w, 3 * w), 1)
    in_window = jnp.abs(kj - w - qi) <= w

    def block(m, carry):
        q0 = pl.multiple_of(m * w, w)
        kpos = t * tq + q0 - w + kj
        valid = in_window & (kpos >= 0) & (kpos < seq)
        for kh in range(SWA_KV_HEADS):
            qs = jnp.concatenate(
                [q_ref[0, pl.ds(q0, w), (kh * groups + g) * hd:(kh * groups + g + 1) * hd] for g in range(groups)],
                axis=0)
            ks = kbuf_ref[pl.ds(q0, 3 * w), kh * hd:(kh + 1) * hd]
            vs = vbuf_ref[pl.ds(q0, 3 * w), kh * hd:(kh + 1) * hd]
            s = lax.dot_general(qs, ks, (((1,), (1,)), ((), ())), preferred_element_type=F32) * scale
            s = jnp.where(valid, s, -jnp.inf)
            row_g = lax.broadcasted_iota(jnp.int32, (groups * w, 1), 0) // w
            sink = jnp.zeros((groups * w, 1), F32)
            for g in range(groups):
                sink = jnp.where(row_g == g, sink_ref[kh * groups + g], sink)
            mx = jnp.maximum(jnp.max(s, axis=-1, keepdims=True), sink)
            p = jnp.exp(s - mx)
            denom = jnp.sum(p, axis=-1, keepdims=True) + jnp.exp(sink - mx)
            o = jnp.dot(p.astype(BF16), vs, preferred_element_type=F32) / denom
            for g in range(groups):
                head = kh * groups + g
                o_ref[0, pl.ds(q0, w), head * hd:(head + 1) * hd] = o[g * w:(g + 1) * w, :].astype(BF16)
        return carry

    lax.fori_loop(0, tq // w, block, 0)


def _swa(proj3, sink, tq):
    bsz, seq, _ = proj3.shape
    w = SWA_WINDOW
    r = tq // w
    last = seq // w - 1
    qb = COL_SQ // SWA_Q_W
    kb, vb = COL_SK // SWA_KV_W, COL_SV // SWA_KV_W

    def neighbours(cb):
        return [
            pl.BlockSpec((1, w, SWA_KV_W), lambda b, t: (b, jnp.maximum(t * r - 1, 0), cb)),
            pl.BlockSpec((1, tq, SWA_KV_W), lambda b, t: (b, t, cb)),
            pl.BlockSpec((1, w, SWA_KV_W), lambda b, t: (b, jnp.minimum((t + 1) * r, last), cb)),
        ]

    return pl.pallas_call(
        _swa_kernel,
        grid=(bsz, seq // tq),
        in_specs=[pl.BlockSpec(memory_space=pltpu.SMEM),
                  pl.BlockSpec((1, tq, SWA_Q_W), lambda b, t: (b, t, qb))] + neighbours(kb) + neighbours(vb),
        out_specs=pl.BlockSpec((1, tq, SWA_Q_W), lambda b, t: (b, t, 0)),
        out_shape=jax.ShapeDtypeStruct((bsz, seq, SWA_Q_W), BF16),
        scratch_shapes=[pltpu.VMEM((tq + 2 * w, SWA_KV_W), BF16), pltpu.VMEM((tq + 2 * w, SWA_KV_W), BF16)],
        compiler_params=_cparams(("parallel", "parallel"), 32),
        name="swa",
    )(sink, proj3, proj3, proj3, proj3, proj3, proj3, proj3)


def _merge_kernel(x_ref, og_ref, os_ref, g0_ref, g1_ref, bm_ref, wg_ref, ws_ref, wo_ref, o_ref):
    a = jnp.dot(og_ref[...], wg_ref[...], preferred_element_type=F32)
    b = jnp.dot(os_ref[...], ws_ref[...], preferred_element_type=F32)
    g0 = jax.nn.sigmoid(g0_ref[...].astype(F32) + bm_ref[0:1, :])
    g1 = jax.nn.sigmoid(g1_ref[...].astype(F32) + bm_ref[1:2, :])
    merged = (g0 * a + g1 * b).astype(BF16)
    o_ref[...] = x_ref[...] + jnp.dot(merged, wo_ref[...], preferred_element_type=F32)


def _resident(shape):
    return pl.BlockSpec(shape, lambda *_: (0,) * len(shape), pipeline_mode=pl.Buffered(1))


def _merge(x2d, o_gla, o_swa, proj, b_merge, w_pg, w_ps, w_out, tm):
    t = x2d.shape[0]
    return pl.pallas_call(
        _merge_kernel,
        grid=(t // tm,),
        in_specs=[
            pl.BlockSpec((tm, D_MODEL), lambda i: (i, 0)),
            pl.BlockSpec((tm, GLA_V_W), lambda i: (i, 0)),
            pl.BlockSpec((tm, SWA_Q_W), lambda i: (i, 0)),
            pl.BlockSpec((tm, D_MODEL), lambda i: (i, 0)),
            pl.BlockSpec((tm, D_MODEL), lambda i: (i, 1)),
            _resident((2, D_MODEL)),
            _resident((GLA_V_W, D_MODEL)),
            _resident((SWA_Q_W, D_MODEL)),
            _resident((D_MODEL, D_MODEL)),
        ],
        out_specs=pl.BlockSpec((tm, D_MODEL), lambda i: (i, 0)),
        out_shape=jax.ShapeDtypeStruct((t, D_MODEL), F32),
        compiler_params=_cparams(("parallel",), 56),
        name="merge",
    )(x2d, o_gla, o_swa, proj, proj, b_merge, w_pg, w_ps, w_out)


def _norm_matmul_kernel(x_ref, g_ref, w_ref, o_ref):
    h = _rms(x_ref[...], g_ref[...]).astype(BF16)
    o_ref[...] = jnp.dot(h, w_ref[...], preferred_element_type=F32).astype(o_ref.dtype)


def _mem_kv(mem2d, g, w_kv, tm):
    t = mem2d.shape[0]
    n = w_kv.shape[1]
    return pl.pallas_call(
        _norm_matmul_kernel,
        grid=(t // tm,),
        in_specs=[
            pl.BlockSpec((tm, D_MODEL), lambda i: (i, 0)),
            _resident((1, D_MODEL)),
            _resident((D_MODEL, n)),
        ],
        out_specs=pl.BlockSpec((tm, n), lambda i: (i, 0)),
        out_shape=jax.ShapeDtypeStruct((t, n), BF16),
        compiler_params=_cparams(("parallel",), 32),
        name="mem_kv",
    )(mem2d, g, w_kv)


def _cross_kernel(x_ref, g_ref, kv_ref, wq_ref, wo_ref, o_ref):
    x = x_ref[0]
    h = _rms(x, g_ref[...]).astype(BF16)
    q = jnp.dot(h, wq_ref[...], preferred_element_type=F32).astype(BF16)
    hd = CROSS_HEAD_DIM
    scale = hd ** -0.5
    outs = []
    for head in range(CROSS_HEADS):
        qh = q[:, head * hd:(head + 1) * hd]
        kh = kv_ref[0, :, head * hd:(head + 1) * hd]
        vh = kv_ref[0, :, CROSS_W + head * hd:CROSS_W + (head + 1) * hd]
        s = lax.dot_general(qh, kh, (((1,), (1,)), ((), ())), preferred_element_type=F32) * scale
        p = jnp.exp(s - jnp.max(s, axis=-1, keepdims=True))
        denom = jnp.sum(p, axis=-1, keepdims=True)
        outs.append((jnp.dot(p.astype(BF16), vh, preferred_element_type=F32) / denom).astype(BF16))
    o = jnp.concatenate(outs, axis=1)
    o_ref[0] = x + jnp.dot(o, wo_ref[...], preferred_element_type=F32)


def _cross(x3, kv3, g, w_q, w_o, tm):
    bsz, seq, _ = x3.shape
    return pl.pallas_call(
        _cross_kernel,
        grid=(bsz, seq // tm),
        in_specs=[
            pl.BlockSpec((1, tm, D_MODEL), lambda b, i: (b, i, 0)),
            _resident((1, D_MODEL)),
            pl.BlockSpec((1, N_MEM, 2 * CROSS_W), lambda b, i: (b, 0, 0)),
            _resident((D_MODEL, CROSS_W)),
            _resident((CROSS_W, D_MODEL)),
        ],
        out_specs=pl.BlockSpec((1, tm, D_MODEL), lambda b, i: (b, i, 0)),
        out_shape=jax.ShapeDtypeStruct((bsz, seq, D_MODEL), F32),
        compiler_params=_cparams(("parallel", "parallel"), 48),
        name="cross_attention",
    )(x3, g, kv3, w_q, w_o)


HALF_D = D_MODEL // 2
ROUTE_TM = 512
DISPATCH_TM = 512
COMBINE_TM = 256
EXPERT_ROWS = 256


def _pack_rows(a):
    return pltpu.pack_elementwise([a[:, :HALF_D], a[:, HALF_D:]], packed_dtype=BF16)


def _unpack_rows(w):
    lo = pltpu.unpack_elementwise(w, index=0, packed_dtype=BF16, unpacked_dtype=F32)
    hi = pltpu.unpack_elementwise(w, index=1, packed_dtype=BF16, unpacked_dtype=F32)
    return jnp.concatenate([lo, hi], axis=1)


def _route_kernel(x_ref, g_ref, wr_ref, br_ref, cnt0_ref, hp_ref, er_ref, gk_ref, comb_ref, cnt_ref, run_ref):
    i = pl.program_id(0)
    tm = x_ref.shape[0]
    lane = lax.broadcasted_iota(jnp.int32, (tm, LANES), 1)

    @pl.when(i == 0)
    def _():
        run_ref[...] = cnt0_ref[...]

    hf = _rms(x_ref[...], g_ref[...])
    h_hi = hf.astype(BF16)
    h_lo = (hf - h_hi.astype(F32)).astype(BF16)
    logits = (jnp.dot(h_hi, wr_ref[0], preferred_element_type=F32)
              + jnp.dot(h_lo, wr_ref[0], preferred_element_type=F32)
              + jnp.dot(h_hi, wr_ref[1], preferred_element_type=F32)) + br_ref[...]
    logits = jnp.where(lane < N_EXPERTS, logits, -jnp.inf)
    vals, hots, idxs = [], [], []
    for _ in range(TOP_K):
        mx = jnp.max(logits, axis=-1, keepdims=True)
        idx = jnp.min(jnp.where(logits == mx, lane, LANES), axis=-1, keepdims=True)
        hot = lane == idx
        vals.append(mx)
        hots.append(hot)
        idxs.append(idx)
        logits = jnp.where(hot, -jnp.inf, logits)
    exps = [jnp.exp(v - vals[0]) for v in vals]
    denom = exps[0] + exps[1] + exps[2] + exps[3]
    sel = jnp.zeros((tm, LANES), F32)
    comb = jnp.zeros((tm, LANES), F32)
    gk = jnp.zeros((tm, LANES), F32)
    for k in range(TOP_K):
        gate = exps[k] / denom
        sel = sel + jnp.where(hots[k], 1.0, 0.0)
        comb = comb + jnp.where(hots[k], gate, 0.0)
        gk = jnp.where(lane == k, gate, gk)
    comb_ref[...] = comb
    gk_ref[...] = gk

    row = lax.broadcasted_iota(jnp.int32, (tm, tm), 0)
    col = lax.broadcasted_iota(jnp.int32, (tm, tm), 1)
    earlier = jnp.where(col < row, 1.0, 0.0).astype(BF16)
    rank = jnp.dot(earlier, sel.astype(BF16), preferred_element_type=F32) + run_ref[...]
    run_ref[...] = run_ref[...] + jnp.sum(sel, axis=0, keepdims=True)
    cnt_ref[...] = run_ref[...]

    table = jnp.zeros((tm, LANES), jnp.int32)
    for k in range(TOP_K):
        rank_k = jnp.sum(jnp.where(hots[k], rank, 0.0), axis=-1, keepdims=True).astype(jnp.int32)
        table = jnp.where(lane == k, idxs[k], table)
        table = jnp.where(lane == TOP_K + k, rank_k, table)
    er_ref[...] = jnp.transpose(table)[:2 * TOP_K, :]
    hp_ref[...] = _pack_rows(hf)


def _route(x2d, g, w_router, b_router, cnt0):
    t = x2d.shape[0]
    tm = ROUTE_TM
    const = lambda *shape: pl.BlockSpec(shape, lambda i: (0,) * len(shape))
    return pl.pallas_call(
        _route_kernel,
        grid=(t // tm,),
        in_specs=[
            pl.BlockSpec((tm, D_MODEL), lambda i: (i, 0)),
            const(1, D_MODEL),
            const(2, D_MODEL, LANES),
            const(1, LANES),
            const(1, LANES),
        ],
        out_specs=[
            pl.BlockSpec((tm, HALF_D), lambda i: (i, 0)),
            pl.BlockSpec((2 * TOP_K, tm), lambda i: (0, i)),
            pl.BlockSpec((tm, LANES), lambda i: (i, 0)),
            pl.BlockSpec((tm, LANES), lambda i: (i, 0)),
            const(1, LANES),
        ],
        out_shape=[
            jax.ShapeDtypeStruct((t, HALF_D), jnp.uint32),
            jax.ShapeDtypeStruct((2 * TOP_K, t), jnp.int32),
            jax.ShapeDtypeStruct((t, LANES), F32),
            jax.ShapeDtypeStruct((t, LANES), F32),
            jax.ShapeDtypeStruct((1, LANES), F32),
        ],
        scratch_shapes=[pltpu.VMEM((1, LANES), F32)],
        compiler_params=_cparams(("arbitrary",), 40),
        name="moe_route",
    )(x2d, g, w_router, b_router, cnt0)


ROW_DMA_UNROLL = 4


def _dispatch_kernel(tv_ref, pos_ref, hp_ref, hs_ref, zero_ref, sem, zero_sem):
    tm = hp_ref.shape[0]

    @pl.when(pl.program_id(0) == 0)
    def _():
        zero_ref[...] = jnp.zeros_like(zero_ref)

        def tile_fill(g):
            r0 = pl.multiple_of(g * EXPERT_ROWS, EXPERT_ROWS)
            return pltpu.make_async_copy(zero_ref, hs_ref.at[pl.ds(r0, EXPERT_ROWS), :], zero_sem)

        def start_fill(g, carry):
            @pl.when(tv_ref[g] < EXPERT_ROWS)
            def _():
                tile_fill(g).start()
            return carry

        def wait_fill(g, carry):
            @pl.when(tv_ref[g] < EXPERT_ROWS)
            def _():
                tile_fill(g).wait()
            return carry

        n_tiles = hs_ref.shape[0] // EXPERT_ROWS
        lax.fori_loop(0, n_tiles, start_fill, 0)
        lax.fori_loop(0, n_tiles, wait_fill, 0)

    def row_copy(t, k):
        return pltpu.make_async_copy(hp_ref.at[pl.ds(t, 1), :], hs_ref.at[pl.ds(pos_ref[k, t], 1), :], sem)

    def issue(t, carry):
        for k in range(TOP_K):
            row_copy(t, k).start()
        return carry

    def drain(t, carry):
        for k in range(TOP_K):
            row_copy(t, k).wait()
        return carry

    lax.fori_loop(0, tm, issue, 0, unroll=ROW_DMA_UNROLL)
    lax.fori_loop(0, tm, drain, 0, unroll=ROW_DMA_UNROLL)


def _dispatch(tile_valid, pos, hp, n_rows):
    t = hp.shape[0]
    tm = DISPATCH_TM
    return pl.pallas_call(
        _dispatch_kernel,
        grid=(t // tm,),
        in_specs=[
            pl.BlockSpec(memory_space=pltpu.SMEM),
            pl.BlockSpec((TOP_K, tm), lambda i: (0, i), memory_space=pltpu.SMEM),
            pl.BlockSpec((tm, HALF_D), lambda i: (i, 0)),
        ],
        out_specs=pl.BlockSpec(memory_space=pl.ANY),
        out_shape=jax.ShapeDtypeStruct((n_rows, HALF_D), jnp.uint32),
        scratch_shapes=[pltpu.VMEM((EXPERT_ROWS, HALF_D), jnp.uint32), pltpu.SemaphoreType.DMA,
                        pltpu.SemaphoreType.DMA],
        compiler_params=_cparams(("arbitrary",), 32),
        name="moe_dispatch",
    )(tile_valid, pos, hp)


def _expert_kernel(te_ref, tv_ref, tb_ref, hs_ref, wup_ref, bup_ref, wdn_ref, ys_ref):
    g = pl.program_id(0)
    valid = tv_ref[g]

    @pl.when(valid == 0)
    def _():
        ys_ref[...] = jnp.zeros_like(ys_ref)

    @pl.when(valid > 0)
    def _():
        x = _unpack_rows(hs_ref[...]).astype(BF16)
        up = jnp.dot(x, wup_ref[0], preferred_element_type=F32) + bup_ref[0]
        glu = jnp.minimum(up[:, :D_FF], SWIGLU_LIMIT)
        lin = jnp.clip(up[:, D_FF:], -SWIGLU_LIMIT, SWIGLU_LIMIT)
        act = glu * jax.nn.sigmoid(SWIGLU_ALPHA * glu) * (lin + 1.0)
        y = jnp.dot(act.astype(BF16), wdn_ref[0], preferred_element_type=F32)
        ys_ref[...] = _pack_rows(y)


def _experts(tile_expert, tile_valid, tile_block, hs, w_up, b_up, w_down):
    n_rows = hs.shape[0]
    rows = EXPERT_ROWS
    grid_spec = pltpu.PrefetchScalarGridSpec(
        num_scalar_prefetch=3,
        grid=(n_rows // rows,),
        in_specs=[
            pl.BlockSpec((rows, HALF_D), lambda g, te, tv, tb: (tb[g], 0)),
            pl.BlockSpec((1, D_MODEL, 2 * D_FF), lambda g, te, tv, tb: (te[g], 0, 0)),
            pl.BlockSpec((1, 1, 2 * D_FF), lambda g, te, tv, tb: (te[g], 0, 0)),
            pl.BlockSpec((1, D_FF, D_MODEL), lambda g, te, tv, tb: (te[g], 0, 0)),
        ],
        out_specs=pl.BlockSpec((rows, HALF_D), lambda g, te, tv, tb: (g, 0)),
    )
    return pl.pallas_call(
        _expert_kernel,
        grid_spec=grid_spec,
        out_shape=jax.ShapeDtypeStruct((n_rows, HALF_D), jnp.uint32),
        compiler_params=_cparams(("arbitrary",), 32),
        name="moe_experts",
    )(tile_expert, tile_valid, tile_block, hs, w_up, b_up, w_down)


def _combine_kernel(pos_ref, x_ref, gk_ref, comb_ref, bdn_ref, gf_ref, ys_ref, o_ref, ybuf_ref, sem):
    tm = x_ref.shape[0]

    def row_copy(t, k):
        return pltpu.make_async_copy(ys_ref.at[pl.ds(pos_ref[k, t], 1), :], ybuf_ref.at[k, pl.ds(t, 1), :], sem)

    def issue(t, carry):
        for k in range(TOP_K):
            row_copy(t, k).start()
        return carry

    def drain(t, carry):
        for k in range(TOP_K):
            row_copy(t, k).wait()
        return carry

    lax.fori_loop(0, tm, issue, 0, unroll=ROW_DMA_UNROLL)
    acc = x_ref[...] + jnp.dot(comb_ref[...].astype(BF16), bdn_ref[...], preferred_element_type=F32)
    lax.fori_loop(0, tm, drain, 0, unroll=ROW_DMA_UNROLL)
    gk = gk_ref[...]
    for k in range(TOP_K):
        acc = acc + gk[:, k:k + 1] * _unpack_rows(ybuf_ref[k])
    o_ref[...] = _rms(acc, gf_ref[...])


def _combine(pos, tok_offset, x2d, gk, comb, b_down, g_final, ys):
    t = x2d.shape[0]
    tm = COMBINE_TM
    off = tok_offset // tm
    const = lambda *shape: pl.BlockSpec(shape, lambda i: (0,) * len(shape))
    return pl.pallas_call(
        _combine_kernel,
        grid=(t // tm,),
        in_specs=[
            pl.BlockSpec((TOP_K, tm), lambda i: (0, off + i), memory_space=pltpu.SMEM),
            pl.BlockSpec((tm, D_MODEL), lambda i: (i, 0)),
            pl.BlockSpec((tm, LANES), lambda i: (i, 0)),
            pl.BlockSpec((tm, LANES), lambda i: (i, 0)),
            const(LANES, D_MODEL),
            const(1, D_MODEL),
            pl.BlockSpec(memory_space=pl.ANY),
        ],
        out_specs=pl.BlockSpec((tm, D_MODEL), lambda i: (i, 0)),
        out_shape=jax.ShapeDtypeStruct((t, D_MODEL), F32),
        scratch_shapes=[pltpu.VMEM((TOP_K, tm, HALF_D), jnp.uint32), pltpu.SemaphoreType.DMA],
        compiler_params=_cparams(("arbitrary",), 32),
        name="moe_combine",
    )(pos, x2d, gk, comb, b_down, g_final, ys)


def _tile_plan(counts, n_tiles):
    rows = EXPERT_ROWS
    tiles_e = (counts + rows - 1) // rows
    tile_end = jnp.cumsum(tiles_e)
    tile_start = tile_end - tiles_e
    total = tile_end[-1]
    g = jnp.arange(n_tiles, dtype=jnp.int32)
    last = total - 1
    g_eff = jnp.minimum(g, last)
    te = jnp.sum((tile_end[None, :] <= g_eff[:, None]).astype(jnp.int32), axis=1)
    te = jnp.minimum(te, N_EXPERTS - 1)
    valid = jnp.clip(counts[te] - (g_eff - tile_start[te]) * rows, 0, rows)
    valid = jnp.where(g < total, valid, 0).astype(jnp.int32)
    return (tile_start * rows).astype(jnp.int32), te, valid, g_eff.astype(jnp.int32)


def _moe(x2_groups, p):
    n_tok = sum(x.shape[0] for x in x2_groups)
    n_tiles = n_tok * TOP_K // EXPERT_ROWS + N_EXPERTS
    cnt = jnp.zeros((1, LANES), F32)
    routed = []
    for x2d in x2_groups:
        hp, er, gk, comb, cnt = _route(x2d, p["norm_moe"], p["w_router"], p["b_router"], cnt)
        routed.append((hp, er, gk, comb))
    counts = cnt[0, :N_EXPERTS].astype(jnp.int32)
    group_start, tile_expert, tile_valid, tile_block = _tile_plan(counts, n_tiles)
    hp_all = jnp.concatenate([r[0] for r in routed], axis=0)
    er_all = jnp.concatenate([r[1] for r in routed], axis=1)
    pos = jnp.take(group_start, er_all[:TOP_K], axis=0) + er_all[TOP_K:]
    hs = _dispatch(tile_valid, pos, hp_all, n_tiles * EXPERT_ROWS)
    ys = _experts(tile_expert, tile_valid, tile_block, hs, p["w_up"], p["b_up"], p["w_down"])
    outs, offset = [], 0
    for x2d, (_, _, gk, comb) in zip(x2_groups, routed):
        outs.append(_combine(pos, offset, x2d, gk, comb, p["b_down"], p["norm_final"], ys))
        offset += x2d.shape[0]
    return outs


def _deinterleave_kernel(w_ref, perm_ref, o_ref):
    o_ref[0] = jnp.dot(w_ref[0].astype(BF16), perm_ref[...], preferred_element_type=F32).astype(BF16)


def _deinterleave_up(w_up):
    n_e, d, n = w_up.shape
    src = jnp.arange(n)
    dst = jnp.where(src % 2 == 0, src // 2, n // 2 + src // 2)
    perm = jnp.zeros((n, n), BF16).at[src, dst].set(1.0)
    return pl.pallas_call(
        _deinterleave_kernel,
        grid=(n_e,),
        in_specs=[pl.BlockSpec((1, d, n), lambda e: (e, 0, 0)), pl.BlockSpec((n, n), lambda e: (0, 0))],
        out_specs=pl.BlockSpec((1, d, n), lambda e: (e, 0, 0)),
        out_shape=jax.ShapeDtypeStruct((n_e, d, n), BF16),
        compiler_params=_cparams(("parallel",), 32),
        name="deinterleave_up",
    )(w_up, perm)


def _rope_tables(seq):
    inv = ROPE_THETA ** (-jnp.arange(ROPE_HALF, dtype=F32) / ROPE_HALF)
    ang = jnp.arange(seq, dtype=F32)[:, None] * inv[None, :]
    cos, sin = jnp.cos(ang), jnp.sin(ang)
    pad = SWA_HEAD_DIM - ROPE_DIM
    cos_t = jnp.concatenate([cos, cos, jnp.ones((seq, pad), F32)], axis=1)
    sin_t = jnp.concatenate([-sin, sin, jnp.zeros((seq, pad), F32)], axis=1)
    return cos_t, sin_t


def _prepare(norm_mix, w_in, w_gla_decay, b_gla_decay, gla_out_norm, swa_sink, b_merge, w_proj_gla, w_proj_swa,
             w_out, norm_cross, norm_mem, w_cross_q, w_cross_kv, w_cross_o, norm_moe, w_router, b_router,
             w_up, b_up, w_down, b_down, norm_final):
    p = {}
    lr0 = 2 * GLA_QK_W + 2 * GLA_V_W
    lr1 = lr0 + 2 * GLA_GATE_RANK
    g0 = lr1 + SWA_Q_W + 2 * SWA_KV_W
    p["w_main"] = jnp.concatenate([w_in[:, g0:], w_in[:, :lr0], w_in[:, lr1:g0]], axis=1).astype(BF16)
    p["w_lr"] = jnp.pad(w_in[:, lr0:lr1], ((0, 0), (0, LANES - 2 * GLA_GATE_RANK))).astype(BF16)
    wdec = jnp.zeros((2, LANES, GLA_QK_W), F32)
    for d in range(2):
        wdec = wdec.at[d, d * GLA_GATE_RANK:(d + 1) * GLA_GATE_RANK, :].set(w_gla_decay[d])
    p["w_dec"] = wdec.astype(BF16)
    p["b_dec"] = b_gla_decay.reshape(2, 1, GLA_QK_W)
    p["g_gla"] = gla_out_norm.reshape(1, GLA_V_W)
    p["sink"] = swa_sink
    p["norm_mix"] = norm_mix.reshape(1, D_MODEL)
    p["b_merge"] = b_merge
    p["w_pg"] = w_proj_gla.astype(BF16)
    p["w_ps"] = w_proj_swa.astype(BF16)
    p["w_out"] = w_out.astype(BF16)
    p["norm_cross"] = norm_cross.reshape(1, D_MODEL)
    p["norm_mem"] = norm_mem.reshape(1, D_MODEL)
    p["w_cq"] = w_cross_q.astype(BF16)
    p["w_ckv"] = w_cross_kv.astype(BF16)
    p["w_co"] = w_cross_o.astype(BF16)
    p["norm_moe"] = norm_moe.reshape(1, D_MODEL)
    wr = jnp.pad(w_router, ((0, 0), (0, LANES - N_EXPERTS)))
    wr_hi = wr.astype(BF16)
    p["w_router"] = jnp.stack([wr_hi, (wr - wr_hi.astype(F32)).astype(BF16)])
    p["b_router"] = jnp.pad(b_router, (0, LANES - N_EXPERTS)).reshape(1, LANES)
    p["w_up"] = _deinterleave_up(w_up)
    b_up3 = b_up.reshape(N_EXPERTS, D_FF, 2)
    p["b_up"] = jnp.concatenate([b_up3[:, :, 0], b_up3[:, :, 1]], axis=1).reshape(N_EXPERTS, 1, 2 * D_FF)
    p["w_down"] = w_down.astype(BF16)
    p["b_down"] = jnp.pad(b_down, ((0, LANES - N_EXPERTS), (0, 0))).astype(BF16)
    p["norm_final"] = norm_final.reshape(1, D_MODEL)
    return p


def _mixer_and_cross(x, mem, p):
    bsz, seq, _ = x.shape
    t = bsz * seq
    x2d = x.reshape(t, D_MODEL)
    cos_t, sin_t = _rope_tables(seq)
    proj, lr = _in_projection(x2d, seq, p["norm_mix"], p["w_main"], p["w_lr"], cos_t, sin_t, min(1024, seq))
    proj3 = proj.reshape(bsz, seq, PROJ_COLS)
    heads_per_step = GLA_HEADS if seq <= 2048 else GLA_HEADS // 2
    o_gla = _gla(proj3, lr.reshape(bsz, seq, LANES), p["w_dec"], p["b_dec"], p["g_gla"], heads_per_step)
    o_swa = _swa(proj3, p["sink"], min(512, seq))
    x1 = _merge(x2d, o_gla.reshape(t, GLA_V_W), o_swa.reshape(t, SWA_Q_W), proj, p["b_merge"],
                p["w_pg"], p["w_ps"], p["w_out"], min(512, seq))
    kv = _mem_kv(mem.reshape(bsz * N_MEM, D_MODEL), p["norm_mem"], p["w_ckv"], N_MEM)
    x2 = _cross(x1.reshape(bsz, seq, D_MODEL), kv.reshape(bsz, N_MEM, 2 * CROSS_W), p["norm_cross"],
                p["w_cq"], p["w_co"], min(512, seq))
    return x2.reshape(t, D_MODEL)


def kernel(x_prompt, x_sample, mem_prompt, mem_sample, norm_mix, w_in, w_gla_decay, b_gla_decay, gla_out_norm,
           swa_sink, b_merge, w_proj_gla, w_proj_swa, w_out, norm_cross, norm_mem, w_cross_q, w_cross_kv,
           w_cross_o, norm_moe, w_router, b_router, w_up, b_up, w_down, b_down, norm_final):
    assert norm_mix.shape[0] == 1, "single-layer stack"
    p = _prepare(norm_mix[0], w_in[0], w_gla_decay[0], b_gla_decay[0], gla_out_norm[0], swa_sink[0], b_merge[0],
                 w_proj_gla[0], w_proj_swa[0], w_out[0], norm_cross[0], norm_mem[0], w_cross_q[0],
                 w_cross_kv[0], w_cross_o[0], norm_moe[0], w_router[0], b_router[0], w_up[0], b_up[0],
                 w_down[0], b_down[0], norm_final)
    groups = ((x_prompt, mem_prompt), (x_sample, mem_sample))
    x2 = [_mixer_and_cross(x, mem, p) for x, mem in groups]
    ys = _moe(x2, p)
    return tuple(y.reshape(x.shape) for y, (x, _) in zip(ys, groups))
```

```python
import functools

import jax
import jax.numpy as jnp
from jax import lax
from jax.experimental import pallas as pl
from jax.experimental.pallas import tpu as pltpu

F32 = jnp.float32
BF16 = jnp.bfloat16

D_MODEL = 2048
EPS = 1e-5
N_MEM = 256

GLA_HEADS = 4
GLA_DK = 128
GLA_DV = 256
GLA_QK_W = GLA_HEADS * GLA_DK
GLA_V_W = GLA_HEADS * GLA_DV
GLA_GATE_RANK = 16
GLA_GATE_NORM = 16.0
GLA_CHUNK = 64
GLA_DECAY_BLOCK = 512

SWA_HEADS = 8
SWA_KV_HEADS = 4
SWA_HEAD_DIM = 128
SWA_WINDOW = 128
SWA_Q_W = SWA_HEADS * SWA_HEAD_DIM
SWA_KV_W = SWA_KV_HEADS * SWA_HEAD_DIM
ROPE_THETA = 500000.0
ROPE_DIM = SWA_HEAD_DIM // 4
ROPE_HALF = ROPE_DIM // 2

CROSS_HEADS = 4
CROSS_HEAD_DIM = 128
CROSS_W = CROSS_HEADS * CROSS_HEAD_DIM

N_EXPERTS = 32
TOP_K = 4
D_FF = D_MODEL // 8
SWIGLU_LIMIT = 7.0
SWIGLU_ALPHA = 1.702

LANES = 128

COL_GATES = 0
COL_GQ = 2 * D_MODEL
COL_GK = COL_GQ + GLA_QK_W
COL_GV = COL_GK + GLA_QK_W
COL_GR = COL_GV + GLA_V_W
COL_SQ = COL_GR + GLA_V_W
COL_SK = COL_SQ + SWA_Q_W
COL_SV = COL_SK + SWA_KV_W
PROJ_COLS = COL_SV + SWA_KV_W
PROJ_TN = 1024
J_SQ = COL_SQ // PROJ_TN
J_SKV = COL_SK // PROJ_TN

MIB = 1024 * 1024


def _cparams(semantics, vmem_mib):
    return pltpu.CompilerParams(dimension_semantics=semantics, vmem_limit_bytes=vmem_mib * MIB)


def _rms(xf, g):
    y = xf * lax.rsqrt(jnp.mean(xf * xf, axis=-1, keepdims=True) + EPS)
    return y * g


def _rope(a, cos_ref, sin_ref):
    width = a.shape[1]
    nh = width // SWA_HEAD_DIM
    cos = jnp.concatenate([cos_ref[...]] * nh, axis=1)
    sin = jnp.concatenate([sin_ref[...]] * nh, axis=1)
    lane = lax.broadcasted_iota(jnp.int32, a.shape, 1) & (SWA_HEAD_DIM - 1)
    partner = jnp.where(lane < ROPE_HALF, pltpu.roll(a, width - ROPE_HALF, 1), pltpu.roll(a, ROPE_HALF, 1))
    return a * cos + partner * sin


def _inproj_kernel(x_ref, g_ref, w_ref, wlr_ref, cos_ref, sin_ref, proj_ref, lr_ref, h_ref):
    j = pl.program_id(1)

    @pl.when(j == 0)
    def _():
        h = _rms(x_ref[...], g_ref[...]).astype(BF16)
        h_ref[...] = h
        lr_ref[...] = jnp.dot(h, wlr_ref[...], preferred_element_type=F32).astype(BF16)

    acc = jnp.dot(h_ref[...], w_ref[...], preferred_element_type=F32)

    @pl.when(j == J_SQ)
    def _():
        proj_ref[...] = _rope(acc, cos_ref, sin_ref).astype(BF16)

    @pl.when(j == J_SKV)
    def _():
        roped = _rope(acc[:, :SWA_KV_W], cos_ref, sin_ref)
        proj_ref[...] = jnp.concatenate([roped, acc[:, SWA_KV_W:]], axis=1).astype(BF16)

    @pl.when(jnp.logical_and(j != J_SQ, j != J_SKV))
    def _():
        proj_ref[...] = acc.astype(BF16)


def _in_projection(x2d, seq, g, w_main, w_lr, cos_t, sin_t, tm):
    t = x2d.shape[0]
    pos_blocks = seq // tm
    return pl.pallas_call(
        _inproj_kernel,
        grid=(t // tm, PROJ_COLS // PROJ_TN),
        in_specs=[
            pl.BlockSpec((tm, D_MODEL), lambda i, j: (i, 0)),
            pl.BlockSpec((1, D_MODEL), lambda i, j: (0, 0)),
            pl.BlockSpec((D_MODEL, PROJ_TN), lambda i, j: (0, j)),
            pl.BlockSpec((D_MODEL, LANES), lambda i, j: (0, 0)),
            pl.BlockSpec((tm, SWA_HEAD_DIM), lambda i, j: (i % pos_blocks, 0)),
            pl.BlockSpec((tm, SWA_HEAD_DIM), lambda i, j: (i % pos_blocks, 0)),
        ],
        out_specs=[
            pl.BlockSpec((tm, PROJ_TN), lambda i, j: (i, j)),
            pl.BlockSpec((tm, LANES), lambda i, j: (i, 0)),
        ],
        out_shape=[
            jax.ShapeDtypeStruct((t, PROJ_COLS), BF16),
            jax.ShapeDtypeStruct((t, LANES), BF16),
        ],
        scratch_shapes=[pltpu.VMEM((tm, D_MODEL), BF16)],
        compiler_params=_cparams(("parallel", "arbitrary"), 52),
        name="in_projection",
    )(x2d, g, w_main, w_lr, cos_t, sin_t)


def _gla_kernel(q_ref, k_ref, v_ref, r_ref, lr_ref, wdec_ref, bdec_ref, gout_ref, o_ref,
                oacc_ref, state_ref, b_ref):
    seq = q_ref.shape[1]
    heads = q_ref.shape[2] // GLA_DK
    c = GLA_CHUNK
    n_chunks = seq // c
    half = n_chunks // 2
    scale = GLA_DK ** -0.5

    state_ref[...] = jnp.zeros_like(state_ref)

    row = lax.broadcasted_iota(jnp.int32, (c, c), 0)
    col = lax.broadcasted_iota(jnp.int32, (c, c), 1)
    cum_mat = (jnp.where(col <= row, 1.0, 0.0).astype(BF16), jnp.where(col >= row, 1.0, 0.0).astype(BF16))
    att_mask = (col <= row, col > row)

    kw = heads * GLA_DK

    def decay_block(j, carry):
        r0 = pl.multiple_of(j * GLA_DECAY_BLOCK, GLA_DECAY_BLOCK)
        lr = lr_ref[0, pl.ds(r0, GLA_DECAY_BLOCK), :]
        zs = [jnp.dot(lr, wdec_ref[d], preferred_element_type=F32) + bdec_ref[d] for d in range(2)]
        for d in range(2):
            log_a = (jnp.minimum(zs[d], 0.0) - jnp.log(1.0 + jnp.exp(-jnp.abs(zs[d])))) / GLA_GATE_NORM
            la_hi = log_a.astype(BF16)
            la_lo = (log_a - la_hi.astype(F32)).astype(BF16)
            for m in range(GLA_DECAY_BLOCK // c):
                sl = slice(m * c, (m + 1) * c)
                b = (jnp.dot(cum_mat[d], la_hi[sl], preferred_element_type=F32)
                     + jnp.dot(cum_mat[d], la_lo[sl], preferred_element_type=F32))
                b_ref[pl.ds(r0 + m * c, c), d * kw:(d + 1) * kw] = b
        return carry

    lax.fori_loop(0, seq // GLA_DECAY_BLOCK, decay_block, 0)

    def advance(i):
        rows = [pl.ds(pl.multiple_of(n * c, c), c) for n in (i, n_chunks - 1 - i)]
        work = []
        for d in range(2):
            b = b_ref[rows[d], d * kw:(d + 1) * kw]
            b_edge = b[c - 1:c, :] if d == 0 else b[0:1, :]
            q = q_ref[0, rows[d], :].astype(F32) * scale
            k = k_ref[0, rows[d], :].astype(F32)
            q_dec = (q * jnp.exp(b)).astype(BF16)
            k_dec = (k * jnp.exp(-b)).astype(BF16)
            k_end = (k * jnp.exp(b_edge - b)).astype(BF16)
            decay = jnp.exp(b_edge)
            for hh in range(heads):
                kc = slice(hh * GLA_DK, (hh + 1) * GLA_DK)
                att = lax.dot_general(q_dec[:, kc], k_dec[:, kc], (((1,), (1,)), ((), ())),
                                      preferred_element_type=F32)
                att = jnp.where(att_mask[d], att, 0.0).astype(BF16)
                work.append((d, hh, q_dec[:, kc], k_end[:, kc], decay[:, kc], att))
        outs = []
        for d, hh, q_dec, k_end, decay, att in work:
            vcols = slice(hh * GLA_DV, (hh + 1) * GLA_DV)
            v = v_ref[0, rows[d], vcols]
            slot = d * heads + hh
            state = state_ref[slot]
            o = jnp.dot(att, v, preferred_element_type=F32)
            o = o + lax.dot_general(q_dec, state.astype(BF16), (((1,), (1,)), ((), ())),
                                    preferred_element_type=F32)
            kv = lax.dot_general(v, k_end, (((0,), (0,)), ((), ())), preferred_element_type=F32)
            state_ref[slot] = decay * state + kv
            outs.append((rows[d], vcols, o))
        return outs

    def finalize(rows, vcols, o):
        o = o * lax.rsqrt(jnp.mean(o * o, axis=-1, keepdims=True) + EPS) * gout_ref[:, vcols]
        r = r_ref[0, rows, vcols].astype(F32)
        o_ref[0, rows, vcols] = (o * (r * jax.nn.sigmoid(r))).astype(BF16)

    def first_half(i, carry):
        for rows, vcols, o in advance(i):
            oacc_ref[rows, vcols] = o
        return carry

    def second_half(i, carry):
        for rows, vcols, o in advance(i):
            finalize(rows, vcols, oacc_ref[rows, vcols] + o)
        return carry

    lax.fori_loop(0, half, first_half, 0)
    lax.fori_loop(half, n_chunks, second_half, 0)


def _gla(proj3, lr3, wdec_pad, bdec, gout, heads_per_step):
    bsz, seq, _ = proj3.shape
    hp = heads_per_step
    kw, vw = hp * GLA_DK, hp * GLA_DV
    qb, kb = COL_GQ // kw, COL_GK // kw
    vb, rb = COL_GV // vw, COL_GR // vw
    return pl.pallas_call(
        _gla_kernel,
        grid=(bsz, GLA_HEADS // hp),
        in_specs=[
            pl.BlockSpec((1, seq, kw), lambda b, h: (b, 0, qb + h)),
            pl.BlockSpec((1, seq, kw), lambda b, h: (b, 0, kb + h)),
            pl.BlockSpec((1, seq, vw), lambda b, h: (b, 0, vb + h)),
            pl.BlockSpec((1, seq, vw), lambda b, h: (b, 0, rb + h)),
            pl.BlockSpec((1, seq, LANES), lambda b, h: (b, 0, 0)),
            pl.BlockSpec((2, LANES, kw), lambda b, h: (0, 0, h)),
            pl.BlockSpec((2, 1, kw), lambda b, h: (0, 0, h)),
            pl.BlockSpec((1, vw), lambda b, h: (0, h)),
        ],
        out_specs=pl.BlockSpec((1, seq, vw), lambda b, h: (b, 0, h)),
        out_shape=jax.ShapeDtypeStruct((bsz, seq, GLA_V_W), BF16),
        scratch_shapes=[
            pltpu.VMEM((seq, vw), F32),
            pltpu.VMEM((2 * hp, GLA_DV, GLA_DK), F32),
            pltpu.VMEM((seq, 2 * kw), F32),
        ],
        compiler_params=_cparams(("parallel", "parallel"), 56),
        name="gla",
    )(proj3, proj3, proj3, proj3, lr3, wdec_pad, bdec, gout)


def _swa_kernel(sink_ref, q_ref, kp_ref, kc_ref, kn_ref, vp_ref, vc_ref, vn_ref, o_ref, kbuf_ref, vbuf_ref):
    t = pl.program_id(1)
    tq = q_ref.shape[1]
    w = SWA_WINDOW
    hd = SWA_HEAD_DIM
    groups = SWA_HEADS // SWA_KV_HEADS
    seq = pl.num_programs(1) * tq
    scale = hd ** -0.5

    kbuf_ref[0:w, :] = kp_ref[0]
    kbuf_ref[w:w + tq, :] = kc_ref[0]
    kbuf_ref[w + tq:, :] = kn_ref[0]
    vbuf_ref[0:w, :] = vp_ref[0]
    vbuf_ref[w:w + tq, :] = vc_ref[0]
    vbuf_ref[w + tq:, :] = vn_ref[0]

    qi = lax.broadcasted_iota(jnp.int32, (groups * w, 3 * w), 0) & (w - 1)
    kj = lax.broadcasted_iota(jnp.int32, (groups * w, 3 * w), 1)
    in_window = jnp.abs(kj - w - qi) <= w

    def block(m, carry):
        q0 = pl.multiple_of(m * w, w)
        kpos = t * tq + q0 - w + kj
        valid = in_window & (kpos >= 0) & (kpos < seq)
        for kh in range(SWA_KV_HEADS):
            qs = jnp.concatenate(
                [q_ref[0, pl.ds(q0, w), (kh * groups + g) * hd:(kh * groups + g + 1) * hd] for g in range(groups)],
                axis=0)
            ks = kbuf_ref[pl.ds(q0, 3 * w), kh * hd:(kh + 1) * hd]
            vs = vbuf_ref[pl.ds(q0, 3 * w), kh * hd:(kh + 1) * hd]
            s = lax.dot_general(qs, ks, (((1,), (1,)), ((), ())), preferred_element_type=F32) * scale
            s = jnp.where(valid, s, -jnp.inf)
            row_g = lax.broadcasted_iota(jnp.int32, (groups * w, 1), 0) // w
            sink = jnp.zeros((groups * w, 1), F32)
            for g in range(groups):
                sink = jnp.where(row_g == g, sink_ref[kh * groups + g], sink)
            mx = jnp.maximum(jnp.max(s, axis=-1, keepdims=True), sink)
            p = jnp.exp(s - mx)
            denom = jnp.sum(p, axis=-1, keepdims=True) + jnp.exp(sink - mx)
            o = jnp.dot(p.astype(BF16), vs, preferred_element_type=F32) / denom
            for g in range(groups):
                head = kh * groups + g
                o_ref[0, pl.ds(q0, w), head * hd:(head + 1) * hd] = o[g * w:(g + 1) * w, :].astype(BF16)
        return carry

    lax.fori_loop(0, tq // w, block, 0)


def _swa(proj3, sink, tq):
    bsz, seq, _ = proj3.shape
    w = SWA_WINDOW
    r = tq // w
    last = seq // w - 1
    qb = COL_SQ // SWA_Q_W
    kb, vb = COL_SK // SWA_KV_W, COL_SV // SWA_KV_W

    def neighbours(cb):
        return [
            pl.BlockSpec((1, w, SWA_KV_W), lambda b, t: (b, jnp.maximum(t * r - 1, 0), cb)),
            pl.BlockSpec((1, tq, SWA_KV_W), lambda b, t: (b, t, cb)),
            pl.BlockSpec((1, w, SWA_KV_W), lambda b, t: (b, jnp.minimum((t + 1) * r, last), cb)),
        ]

    return pl.pallas_call(
        _swa_kernel,
        grid=(bsz, seq // tq),
        in_specs=[pl.BlockSpec(memory_space=pltpu.SMEM),
                  pl.BlockSpec((1, tq, SWA_Q_W), lambda b, t: (b, t, qb))] + neighbours(kb) + neighbours(vb),
        out_specs=pl.BlockSpec((1, tq, SWA_Q_W), lambda b, t: (b, t, 0)),
        out_shape=jax.ShapeDtypeStruct((bsz, seq, SWA_Q_W), BF16),
        scratch_shapes=[pltpu.VMEM((tq + 2 * w, SWA_KV_W), BF16), pltpu.VMEM((tq + 2 * w, SWA_KV_W), BF16)],
        compiler_params=_cparams(("parallel", "parallel"), 32),
        name="swa",
    )(sink, proj3, proj3, proj3, proj3, proj3, proj3, proj3)


def _merge_kernel(x_ref, og_ref, os_ref, g0_ref, g1_ref, bm_ref, wg_ref, ws_ref, wo_ref, o_ref):
    a = jnp.dot(og_ref[...], wg_ref[...], preferred_element_type=F32)
    b = jnp.dot(os_ref[...], ws_ref[...], preferred_element_type=F32)
    g0 = jax.nn.sigmoid(g0_ref[...].astype(F32) + bm_ref[0:1, :])
    g1 = jax.nn.sigmoid(g1_ref[...].astype(F32) + bm_ref[1:2, :])
    merged = (g0 * a + g1 * b).astype(BF16)
    o_ref[...] = x_ref[...] + jnp.dot(merged, wo_ref[...], preferred_element_type=F32)


def _resident(shape):
    return pl.BlockSpec(shape, lambda *_: (0,) * len(shape), pipeline_mode=pl.Buffered(1))


def _merge(x2d, o_gla, o_swa, proj, b_merge, w_pg, w_ps, w_out, tm):
    t = x2d.shape[0]
    return pl.pallas_call(
        _merge_kernel,
        grid=(t // tm,),
        in_specs=[
            pl.BlockSpec((tm, D_MODEL), lambda i: (i, 0)),
            pl.BlockSpec((tm, GLA_V_W), lambda i: (i, 0)),
            pl.BlockSpec((tm, SWA_Q_W), lambda i: (i, 0)),
            pl.BlockSpec((tm, D_MODEL), lambda i: (i, 0)),
            pl.BlockSpec((tm, D_MODEL), lambda i: (i, 1)),
            _resident((2, D_MODEL)),
            _resident((GLA_V_W, D_MODEL)),
            _resident((SWA_Q_W, D_MODEL)),
            _resident((D_MODEL, D_MODEL)),
        ],
        out_specs=pl.BlockSpec((tm, D_MODEL), lambda i: (i, 0)),
        out_shape=jax.ShapeDtypeStruct((t, D_MODEL), F32),
        compiler_params=_cparams(("parallel",), 56),
        name="merge",
    )(x2d, o_gla, o_swa, proj, proj, b_merge, w_pg, w_ps, w_out)


def _norm_matmul_kernel(x_ref, g_ref, w_ref, o_ref):
    h = _rms(x_ref[...], g_ref[...]).astype(BF16)
    o_ref[...] = jnp.dot(h, w_ref[...], preferred_element_type=F32).astype(o_ref.dtype)


def _mem_kv(mem2d, g, w_kv, tm):
    t = mem2d.shape[0]
    n = w_kv.shape[1]
    return pl.pallas_call(
        _norm_matmul_kernel,
        grid=(t // tm,),
        in_specs=[
            pl.BlockSpec((tm, D_MODEL), lambda i: (i, 0)),
            _resident((1, D_MODEL)),
            _resident((D_MODEL, n)),
        ],
        out_specs=pl.BlockSpec((tm, n), lambda i: (i, 0)),
        out_shape=jax.ShapeDtypeStruct((t, n), BF16),
        compiler_params=_cparams(("parallel",), 32),
        name="mem_kv",
    )(mem2d, g, w_kv)


def _cross_kernel(x_ref, g_ref, kv_ref, wq_ref, wo_ref, o_ref):
    x = x_ref[0]
    h = _rms(x, g_ref[...]).astype(BF16)
    q = jnp.dot(h, wq_ref[...], preferred_element_type=F32).astype(BF16)
    hd = CROSS_HEAD_DIM
    scale = hd ** -0.5
    outs = []
    for head in range(CROSS_HEADS):
        qh = q[:, head * hd:(head + 1) * hd]
        kh = kv_ref[0, :, head * hd:(head + 1) * hd]
        vh = kv_ref[0, :, CROSS_W + head * hd:CROSS_W + (head + 1) * hd]
        s = lax.dot_general(qh, kh, (((1,), (1,)), ((), ())), preferred_element_type=F32) * scale
        p = jnp.exp(s - jnp.max(s, axis=-1, keepdims=True))
        denom = jnp.sum(p, axis=-1, keepdims=True)
        outs.append((jnp.dot(p.astype(BF16), vh, preferred_element_type=F32) / denom).astype(BF16))
    o = jnp.concatenate(outs, axis=1)
    o_ref[0] = x + jnp.dot(o, wo_ref[...], preferred_element_type=F32)


def _cross(x3, kv3, g, w_q, w_o, tm):
    bsz, seq, _ = x3.shape
    return pl.pallas_call(
        _cross_kernel,
        grid=(bsz, seq // tm),
        in_specs=[
            pl.BlockSpec((1, tm, D_MODEL), lambda b, i: (b, i, 0)),
            _resident((1, D_MODEL)),
            pl.BlockSpec((1, N_MEM, 2 * CROSS_W), lambda b, i: (b, 0, 0)),
            _resident((D_MODEL, CROSS_W)),
            _resident((CROSS_W, D_MODEL)),
        ],
        out_specs=pl.BlockSpec((1, tm, D_MODEL), lambda b, i: (b, i, 0)),
        out_shape=jax.ShapeDtypeStruct((bsz, seq, D_MODEL), F32),
        compiler_params=_cparams(("parallel", "parallel"), 48),
        name="cross_attention",
    )(x3, g, kv3, w_q, w_o)


HALF_D = D_MODEL // 2
ROUTE_TM = 512
DISPATCH_TM = 512
COMBINE_TM = 256
EXPERT_ROWS = 512


def _pack_rows(a):
    return pltpu.pack_elementwise([a[:, :HALF_D], a[:, HALF_D:]], packed_dtype=BF16)


def _unpack_rows(w):
    lo = pltpu.unpack_elementwise(w, index=0, packed_dtype=BF16, unpacked_dtype=F32)
    hi = pltpu.unpack_elementwise(w, index=1, packed_dtype=BF16, unpacked_dtype=F32)
    return jnp.concatenate([lo, hi], axis=1)


def _route_kernel(x_ref, g_ref, wr_ref, br_ref, cnt0_ref, hp_ref, er_ref, gk_ref, comb_ref, cnt_ref, run_ref):
    i = pl.program_id(0)
    tm = x_ref.shape[0]
    lane = lax.broadcasted_iota(jnp.int32, (tm, LANES), 1)

    @pl.when(i == 0)
    def _():
        run_ref[...] = cnt0_ref[...]

    hf = _rms(x_ref[...], g_ref[...])
    h_hi = hf.astype(BF16)
    h_lo = (hf - h_hi.astype(F32)).astype(BF16)
    logits = (jnp.dot(h_hi, wr_ref[0], preferred_element_type=F32)
              + jnp.dot(h_lo, wr_ref[0], preferred_element_type=F32)
              + jnp.dot(h_hi, wr_ref[1], preferred_element_type=F32)) + br_ref[...]
    logits = jnp.where(lane < N_EXPERTS, logits, -jnp.inf)
    vals, hots, idxs = [], [], []
    for _ in range(TOP_K):
        mx = jnp.max(logits, axis=-1, keepdims=True)
        idx = jnp.min(jnp.where(logits == mx, lane, LANES), axis=-1, keepdims=True)
        hot = lane == idx
        vals.append(mx)
        hots.append(hot)
        idxs.append(idx)
        logits = jnp.where(hot, -jnp.inf, logits)
    exps = [jnp.exp(v - vals[0]) for v in vals]
    denom = exps[0] + exps[1] + exps[2] + exps[3]
    sel = jnp.zeros((tm, LANES), F32)
    comb = jnp.zeros((tm, LANES), F32)
    gk = jnp.zeros((tm, LANES), F32)
    for k in range(TOP_K):
        gate = exps[k] / denom
        sel = sel + jnp.where(hots[k], 1.0, 0.0)
        comb = comb + jnp.where(hots[k], gate, 0.0)
        gk = jnp.where(lane == k, gate, gk)
    comb_ref[...] = comb
    gk_ref[...] = gk

    row = lax.broadcasted_iota(jnp.int32, (tm, tm), 0)
    col = lax.broadcasted_iota(jnp.int32, (tm, tm), 1)
    earlier = jnp.where(col < row, 1.0, 0.0).astype(BF16)
    rank = jnp.dot(earlier, sel.astype(BF16), preferred_element_type=F32) + run_ref[...]
    run_ref[...] = run_ref[...] + jnp.sum(sel, axis=0, keepdims=True)
    cnt_ref[...] = run_ref[...]

    table = jnp.zeros((tm, LANES), jnp.int32)
    for k in range(TOP_K):
        rank_k = jnp.sum(jnp.where(hots[k], rank, 0.0), axis=-1, keepdims=True).astype(jnp.int32)
        table = jnp.where(lane == k, idxs[k], table)
        table = jnp.where(lane == TOP_K + k, rank_k, table)
    er_ref[...] = jnp.transpose(table)[:2 * TOP_K, :]
    hp_ref[...] = _pack_rows(hf)


def _route(x2d, g, w_router, b_router, cnt0):
    t = x2d.shape[0]
    tm = ROUTE_TM
    const = lambda *shape: pl.BlockSpec(shape, lambda i: (0,) * len(shape))
    return pl.pallas_call(
        _route_kernel,
        grid=(t // tm,),
        in_specs=[
            pl.BlockSpec((tm, D_MODEL), lambda i: (i, 0)),
            const(1, D_MODEL),
            const(2, D_MODEL, LANES),
            const(1, LANES),
            const(1, LANES),
        ],
        out_specs=[
            pl.BlockSpec((tm, HALF_D), lambda i: (i, 0)),
            pl.BlockSpec((2 * TOP_K, tm), lambda i: (0, i)),
            pl.BlockSpec((tm, LANES), lambda i: (i, 0)),
            pl.BlockSpec((tm, LANES), lambda i: (i, 0)),
            const(1, LANES),
        ],
        out_shape=[
            jax.ShapeDtypeStruct((t, HALF_D), jnp.uint32),
            jax.ShapeDtypeStruct((2 * TOP_K, t), jnp.int32),
            jax.ShapeDtypeStruct((t, LANES), F32),
            jax.ShapeDtypeStruct((t, LANES), F32),
            jax.ShapeDtypeStruct((1, LANES), F32),
        ],
        scratch_shapes=[pltpu.VMEM((1, LANES), F32)],
        compiler_params=_cparams(("arbitrary",), 40),
        name="moe_route",
    )(x2d, g, w_router, b_router, cnt0)


ROW_DMA_UNROLL = 4


def _dispatch_kernel(tv_ref, pos_ref, hp_ref, hs_ref, zero_ref, sem, zero_sem):
    tm = hp_ref.shape[0]

    @pl.when(pl.program_id(0) == 0)
    def _():
        zero_ref[...] = _pack_rows(jnp.zeros((zero_ref.shape[0], D_MODEL), F32))

        def tile_fill(g):
            r0 = pl.multiple_of(g * EXPERT_ROWS, EXPERT_ROWS)
            return pltpu.make_async_copy(zero_ref, hs_ref.at[pl.ds(r0, EXPERT_ROWS), :], zero_sem)

        def start_fill(g, carry):
            @pl.when(tv_ref[g] < EXPERT_ROWS)
            def _():
                tile_fill(g).start()
            return carry

        def wait_fill(g, carry):
            @pl.when(tv_ref[g] < EXPERT_ROWS)
            def _():
                tile_fill(g).wait()
            return carry

        n_tiles = hs_ref.shape[0] // EXPERT_ROWS
        lax.fori_loop(0, n_tiles, start_fill, 0)
        lax.fori_loop(0, n_tiles, wait_fill, 0)

    def row_copy(t, k):
        return pltpu.make_async_copy(hp_ref.at[pl.ds(t, 1), :], hs_ref.at[pl.ds(pos_ref[k, t], 1), :], sem)

    def issue(t, carry):
        for k in range(TOP_K):
            row_copy(t, k).start()
        return carry

    def drain(t, carry):
        for k in range(TOP_K):
            row_copy(t, k).wait()
        return carry

    lax.fori_loop(0, tm, issue, 0, unroll=ROW_DMA_UNROLL)
    lax.fori_loop(0, tm, drain, 0, unroll=ROW_DMA_UNROLL)


def _dispatch(tile_valid, pos, hp, n_rows):
    t = hp.shape[0]
    tm = DISPATCH_TM
    return pl.pallas_call(
        _dispatch_kernel,
        grid=(t // tm,),
        in_specs=[
            pl.BlockSpec(memory_space=pltpu.SMEM),
            pl.BlockSpec((TOP_K, tm), lambda i: (0, i), memory_space=pltpu.SMEM),
            pl.BlockSpec((tm, HALF_D), lambda i: (i, 0)),
        ],
        out_specs=pl.BlockSpec(memory_space=pl.ANY),
        out_shape=jax.ShapeDtypeStruct((n_rows, HALF_D), jnp.uint32),
        scratch_shapes=[pltpu.VMEM((EXPERT_ROWS, HALF_D), jnp.uint32), pltpu.SemaphoreType.DMA,
                        pltpu.SemaphoreType.DMA],
        compiler_params=_cparams(("arbitrary",), 32),
        name="moe_dispatch",
    )(tile_valid, pos, hp)


def _expert_kernel(te_ref, tv_ref, tb_ref, hs_ref, wup_ref, bup_ref, wdn_ref, ys_ref):
    g = pl.program_id(0)
    valid = tv_ref[g]

    @pl.when(valid == 0)
    def _():
        ys_ref[...] = _pack_rows(jnp.zeros((ys_ref.shape[0], D_MODEL), F32))

    @pl.when(valid > 0)
    def _():
        x = _unpack_rows(hs_ref[...]).astype(BF16)
        up = jnp.dot(x, wup_ref[0], preferred_element_type=F32) + bup_ref[0]
        glu = jnp.minimum(up[:, :D_FF], SWIGLU_LIMIT)
        lin = jnp.clip(up[:, D_FF:], -SWIGLU_LIMIT, SWIGLU_LIMIT)
        act = glu * jax.nn.sigmoid(SWIGLU_ALPHA * glu) * (lin + 1.0)
        y = jnp.dot(act.astype(BF16), wdn_ref[0], preferred_element_type=F32)
        ys_ref[...] = _pack_rows(y)


def _experts(tile_expert, tile_valid, tile_block, hs, w_up, b_up, w_down):
    n_rows = hs.shape[0]
    rows = EXPERT_ROWS
    grid_spec = pltpu.PrefetchScalarGridSpec(
        num_scalar_prefetch=3,
        grid=(n_rows // rows,),
        in_specs=[
            pl.BlockSpec((rows, HALF_D), lambda g, te, tv, tb: (tb[g], 0)),
            pl.BlockSpec((1, D_MODEL, 2 * D_FF), lambda g, te, tv, tb: (te[g], 0, 0)),
            pl.BlockSpec((1, 1, 2 * D_FF), lambda g, te, tv, tb: (te[g], 0, 0)),
            pl.BlockSpec((1, D_FF, D_MODEL), lambda g, te, tv, tb: (te[g], 0, 0)),
        ],
        out_specs=pl.BlockSpec((rows, HALF_D), lambda g, te, tv, tb: (g, 0)),
    )
    return pl.pallas_call(
        _expert_kernel,
        grid_spec=grid_spec,
        out_shape=jax.ShapeDtypeStruct((n_rows, HALF_D), jnp.uint32),
        compiler_params=_cparams(("arbitrary",), 32),
        name="moe_experts",
    )(tile_expert, tile_valid, tile_block, hs, w_up, b_up, w_down)


COMBINE_GROUP = 8


def _combine_kernel(pos_ref, next_pos_ref, x_ref, gk_ref, comb_ref, bdn_ref, gf_ref, ys_ref, o_ref,
                    ybuf0_ref, ybuf1_ref, acc_ref, sem0, sem1):
    i = pl.program_id(0)
    last = pl.num_programs(0) - 1
    tm = x_ref.shape[0]

    def step(own_buf, own_sem, spare_buf, spare_sem):
        def row_copy(p_ref, buf, sem, t, k):
            return pltpu.make_async_copy(ys_ref.at[pl.ds(p_ref[k, t], 1), :], buf.at[k, pl.ds(t, 1), :], sem)

        def start_own(t, carry):
            for k in range(TOP_K):
                row_copy(pos_ref, own_buf, own_sem, t, k).start()
            return carry

        def wait_own(t, carry):
            for k in range(TOP_K):
                row_copy(pos_ref, own_buf, own_sem, t, k).wait()
            return carry

        def wait_spare(t, carry):
            for k in range(TOP_K):
                row_copy(next_pos_ref, spare_buf, spare_sem, t, k).wait()
            return carry

        @pl.when(i == 0)
        def _():
            lax.fori_loop(0, tm, start_own, 0, unroll=ROW_DMA_UNROLL)

        acc_ref[...] = x_ref[...] + jnp.dot(comb_ref[...].astype(BF16), bdn_ref[...],
                                            preferred_element_type=F32)
        lax.fori_loop(0, tm, wait_own, 0, unroll=ROW_DMA_UNROLL)

        def group(j, carry):
            r0 = pl.multiple_of(j * COMBINE_GROUP, COMBINE_GROUP)
            rows = pl.ds(r0, COMBINE_GROUP)
            acc = acc_ref[rows, :]
            gk = gk_ref[rows, :]
            for k in range(TOP_K):
                acc = acc + gk[:, k:k + 1] * _unpack_rows(own_buf[k, rows, :])
            o_ref[rows, :] = _rms(acc, gf_ref[...])
            for t in range(COMBINE_GROUP):
                for k in range(TOP_K):
                    row_copy(next_pos_ref, spare_buf, spare_sem, r0 + t, k).start()
            return carry

        lax.fori_loop(0, tm // COMBINE_GROUP, group, 0)

        @pl.when(i == last)
        def _():
            lax.fori_loop(0, tm, wait_spare, 0, unroll=ROW_DMA_UNROLL)

    @pl.when(lax.rem(i, 2) == 0)
    def _():
        step(ybuf0_ref, sem0, ybuf1_ref, sem1)

    @pl.when(lax.rem(i, 2) == 1)
    def _():
        step(ybuf1_ref, sem1, ybuf0_ref, sem0)


def _combine(pos, tok_offset, x2d, gk, comb, b_down, g_final, ys):
    t = x2d.shape[0]
    tm = COMBINE_TM
    off = tok_offset // tm
    last = t // tm - 1
    const = lambda *shape: pl.BlockSpec(shape, lambda i: (0,) * len(shape))
    return pl.pallas_call(
        _combine_kernel,
        grid=(t // tm,),
        in_specs=[
            pl.BlockSpec((TOP_K, tm), lambda i: (0, off + i), memory_space=pltpu.SMEM),
            pl.BlockSpec((TOP_K, tm), lambda i: (0, off + jnp.minimum(i + 1, last)), memory_space=pltpu.SMEM),
            pl.BlockSpec((tm, D_MODEL), lambda i: (i, 0)),
            pl.BlockSpec((tm, LANES), lambda i: (i, 0)),
            pl.BlockSpec((tm, LANES), lambda i: (i, 0)),
            const(LANES, D_MODEL),
            const(1, D_MODEL),
            pl.BlockSpec(memory_space=pl.ANY),
        ],
        out_specs=pl.BlockSpec((tm, D_MODEL), lambda i: (i, 0)),
        out_shape=jax.ShapeDtypeStruct((t, D_MODEL), F32),
        scratch_shapes=[pltpu.VMEM((TOP_K, tm, HALF_D), jnp.uint32), pltpu.VMEM((TOP_K, tm, HALF_D), jnp.uint32),
                        pltpu.VMEM((tm, D_MODEL), F32), pltpu.SemaphoreType.DMA, pltpu.SemaphoreType.DMA],
        compiler_params=_cparams(("arbitrary",), 32),
        name="moe_combine",
    )(pos, pos, x2d, gk, comb, b_down, g_final, ys)


def _tile_plan(counts, n_tiles):
    rows = EXPERT_ROWS
    tiles_e = (counts + rows - 1) // rows
    tile_end = jnp.cumsum(tiles_e)
    tile_start = tile_end - tiles_e
    total = tile_end[-1]
    g = jnp.arange(n_tiles, dtype=jnp.int32)
    last = total - 1
    g_eff = jnp.minimum(g, last)
    te = jnp.sum((tile_end[None, :] <= g_eff[:, None]).astype(jnp.int32), axis=1)
    te = jnp.minimum(te, N_EXPERTS - 1)
    valid = jnp.clip(counts[te] - (g_eff - tile_start[te]) * rows, 0, rows)
    valid = jnp.where(g < total, valid, 0).astype(jnp.int32)
    return (tile_start * rows).astype(jnp.int32), te, valid, g_eff.astype(jnp.int32)


def _moe(x2_groups, p):
    n_tok = sum(x.shape[0] for x in x2_groups)
    n_tiles = n_tok * TOP_K // EXPERT_ROWS + N_EXPERTS
    cnt = jnp.zeros((1, LANES), F32)
    routed = []
    for x2d in x2_groups:
        hp, er, gk, comb, cnt = _route(x2d, p["norm_moe"], p["w_router"], p["b_router"], cnt)
        routed.append((hp, er, gk, comb))
    counts = cnt[0, :N_EXPERTS].astype(jnp.int32)
    group_start, tile_expert, tile_valid, tile_block = _tile_plan(counts, n_tiles)
    hp_all = jnp.concatenate([r[0] for r in routed], axis=0)
    er_all = jnp.concatenate([r[1] for r in routed], axis=1)
    experts = jnp.arange(N_EXPERTS, dtype=jnp.int32)[:, None, None]
    pos = jnp.sum(jnp.where(er_all[None, :TOP_K] == experts, group_start[:, None, None], 0), axis=0) + er_all[TOP_K:]
    hs = _dispatch(tile_valid, pos, hp_all, n_tiles * EXPERT_ROWS)
    ys = _experts(tile_expert, tile_valid, tile_block, hs, p["w_up"], p["b_up"], p["w_down"])
    outs, offset = [], 0
    for x2d, (_, _, gk, comb) in zip(x2_groups, routed):
        outs.append(_combine(pos, offset, x2d, gk, comb, p["b_down"], p["norm_final"], ys))
        offset += x2d.shape[0]
    return outs


def _deinterleave_kernel(w_ref, perm_ref, o_ref):
    o_ref[0] = jnp.dot(w_ref[0].astype(BF16), perm_ref[...], preferred_element_type=F32).astype(BF16)


def _deinterleave_up(w_up):
    n_e, d, n = w_up.shape
    src = jnp.arange(n)
    dst = jnp.where(src % 2 == 0, src // 2, n // 2 + src // 2)
    perm = jnp.zeros((n, n), BF16).at[src, dst].set(1.0)
    return pl.pallas_call(
        _deinterleave_kernel,
        grid=(n_e,),
        in_specs=[pl.BlockSpec((1, d, n), lambda e: (e, 0, 0)), pl.BlockSpec((n, n), lambda e: (0, 0))],
        out_specs=pl.BlockSpec((1, d, n), lambda e: (e, 0, 0)),
        out_shape=jax.ShapeDtypeStruct((n_e, d, n), BF16),
        compiler_params=_cparams(("parallel",), 32),
        name="deinterleave_up",
    )(w_up, perm)


def _rope_tables(seq):
    inv = ROPE_THETA ** (-jnp.arange(ROPE_HALF, dtype=F32) / ROPE_HALF)
    ang = jnp.arange(seq, dtype=F32)[:, None] * inv[None, :]
    cos, sin = jnp.cos(ang), jnp.sin(ang)
    pad = SWA_HEAD_DIM - ROPE_DIM
    cos_t = jnp.concatenate([cos, cos, jnp.ones((seq, pad), F32)], axis=1)
    sin_t = jnp.concatenate([-sin, sin, jnp.zeros((seq, pad), F32)], axis=1)
    return cos_t, sin_t


def _prepare(norm_mix, w_in, w_gla_decay, b_gla_decay, gla_out_norm, swa_sink, b_merge, w_proj_gla, w_proj_swa,
             w_out, norm_cross, norm_mem, w_cross_q, w_cross_kv, w_cross_o, norm_moe, w_router, b_router,
             w_up, b_up, w_down, b_down, norm_final):
    p = {}
    lr0 = 2 * GLA_QK_W + 2 * GLA_V_W
    lr1 = lr0 + 2 * GLA_GATE_RANK
    g0 = lr1 + SWA_Q_W + 2 * SWA_KV_W
    p["w_main"] = jnp.concatenate([w_in[:, g0:], w_in[:, :lr0], w_in[:, lr1:g0]], axis=1).astype(BF16)
    p["w_lr"] = jnp.pad(w_in[:, lr0:lr1], ((0, 0), (0, LANES - 2 * GLA_GATE_RANK))).astype(BF16)
    wdec = jnp.zeros((2, LANES, GLA_QK_W), F32)
    for d in range(2):
        wdec = wdec.at[d, d * GLA_GATE_RANK:(d + 1) * GLA_GATE_RANK, :].set(w_gla_decay[d])
    p["w_dec"] = wdec.astype(BF16)
    p["b_dec"] = b_gla_decay.reshape(2, 1, GLA_QK_W)
    p["g_gla"] = gla_out_norm.reshape(1, GLA_V_W)
    p["sink"] = swa_sink
    p["norm_mix"] = norm_mix.reshape(1, D_MODEL)
    p["b_merge"] = b_merge
    p["w_pg"] = w_proj_gla.astype(BF16)
    p["w_ps"] = w_proj_swa.astype(BF16)
    p["w_out"] = w_out.astype(BF16)
    p["norm_cross"] = norm_cross.reshape(1, D_MODEL)
    p["norm_mem"] = norm_mem.reshape(1, D_MODEL)
    p["w_cq"] = w_cross_q.astype(BF16)
    p["w_ckv"] = w_cross_kv.astype(BF16)
    p["w_co"] = w_cross_o.astype(BF16)
    p["norm_moe"] = norm_moe.reshape(1, D_MODEL)
    wr = jnp.pad(w_router, ((0, 0), (0, LANES - N_EXPERTS)))
    wr_hi = wr.astype(BF16)
    p["w_router"] = jnp.stack([wr_hi, (wr - wr_hi.astype(F32)).astype(BF16)])
    p["b_router"] = jnp.pad(b_router, (0, LANES - N_EXPERTS)).reshape(1, LANES)
    p["w_up"] = _deinterleave_up(w_up)
    b_up3 = b_up.reshape(N_EXPERTS, D_FF, 2)
    p["b_up"] = jnp.concatenate([b_up3[:, :, 0], b_up3[:, :, 1]], axis=1).reshape(N_EXPERTS, 1, 2 * D_FF)
    p["w_down"] = w_down.astype(BF16)
    p["b_down"] = jnp.pad(b_down, ((0, LANES - N_EXPERTS), (0, 0))).astype(BF16)
    p["norm_final"] = norm_final.reshape(1, D_MODEL)
    return p


def _mixer_and_cross(x, mem, p):
    bsz, seq, _ = x.shape
    t = bsz * seq
    x2d = x.reshape(t, D_MODEL)
    cos_t, sin_t = _rope_tables(seq)
    proj, lr = _in_projection(x2d, seq, p["norm_mix"], p["w_main"], p["w_lr"], cos_t, sin_t, min(1024, seq))
    proj3 = proj.reshape(bsz, seq, PROJ_COLS)
    heads_per_step = GLA_HEADS if seq <= 2048 else GLA_HEADS // 2
    o_gla = _gla(proj3, lr.reshape(bsz, seq, LANES), p["w_dec"], p["b_dec"], p["g_gla"], heads_per_step)
    o_swa = _swa(proj3, p["sink"], min(512, seq))
    x1 = _merge(x2d, o_gla.reshape(t, GLA_V_W), o_swa.reshape(t, SWA_Q_W), proj, p["b_merge"],
                p["w_pg"], p["w_ps"], p["w_out"], min(512, seq))
    kv = _mem_kv(mem.reshape(bsz * N_MEM, D_MODEL), p["norm_mem"], p["w_ckv"], N_MEM)
    x2 = _cross(x1.reshape(bsz, seq, D_MODEL), kv.reshape(bsz, N_MEM, 2 * CROSS_W), p["norm_cross"],
                p["w_cq"], p["w_co"], min(512, seq))
    return x2.reshape(t, D_MODEL)


def kernel(x_prompt, x_sample, mem_prompt, mem_sample, norm_mix, w_in, w_gla_decay, b_gla_decay, gla_out_norm,
           swa_sink, b_merge, w_proj_gla, w_proj_swa, w_out, norm_cross, norm_mem, w_cross_q, w_cross_kv,
           w_cross_o, norm_moe, w_router, b_router, w_up, b_up, w_down, b_down, norm_final):
    assert norm_mix.shape[0] == 1, "single-layer stack"
    p = _prepare(norm_mix[0], w_in[0], w_gla_decay[0], b_gla_decay[0], gla_out_norm[0], swa_sink[0], b_merge[0],
                 w_proj_gla[0], w_proj_swa[0], w_out[0], norm_cross[0], norm_mem[0], w_cross_q[0],
                 w_cross_kv[0], w_cross_o[0], norm_moe[0], w_router[0], b_router[0], w_up[0], b_up[0],
                 w_down[0], b_down[0], norm_final)
    groups = ((x_prompt, mem_prompt), (x_sample, mem_sample))
    x2 = [_mixer_and_cross(x, mem, p) for x, mem in groups]
    ys = _moe(x2, p)
    return tuple(y.reshape(x.shape) for y, (x, _) in zip(ys, groups))
```

```python
import functools

import jax
import jax.numpy as jnp
from jax import lax
from jax.experimental import pallas as pl
from jax.experimental.pallas import tpu as pltpu

F32 = jnp.float32
BF16 = jnp.bfloat16

D_MODEL = 2048
EPS = 1e-5
N_MEM = 256

GLA_HEADS = 4
GLA_DK = 128
GLA_DV = 256
GLA_QK_W = GLA_HEADS * GLA_DK
GLA_V_W = GLA_HEADS * GLA_DV
GLA_GATE_RANK = 16
GLA_GATE_NORM = 16.0
GLA_CHUNK = 64
GLA_DECAY_BLOCK = 512

SWA_HEADS = 8
SWA_KV_HEADS = 4
SWA_HEAD_DIM = 128
SWA_WINDOW = 128
SWA_Q_W = SWA_HEADS * SWA_HEAD_DIM
SWA_KV_W = SWA_KV_HEADS * SWA_HEAD_DIM
ROPE_THETA = 500000.0
ROPE_DIM = SWA_HEAD_DIM // 4
ROPE_HALF = ROPE_DIM // 2

CROSS_HEADS = 4
CROSS_HEAD_DIM = 128
CROSS_W = CROSS_HEADS * CROSS_HEAD_DIM

N_EXPERTS = 32
TOP_K = 4
D_FF = D_MODEL // 8
SWIGLU_LIMIT = 7.0
SWIGLU_ALPHA = 1.702

LANES = 128

COL_GATES = 0
COL_GQ = 2 * D_MODEL
COL_GK = COL_GQ + GLA_QK_W
COL_GV = COL_GK + GLA_QK_W
COL_GR = COL_GV + GLA_V_W
COL_SQ = COL_GR + GLA_V_W
COL_SK = COL_SQ + SWA_Q_W
COL_SV = COL_SK + SWA_KV_W
PROJ_COLS = COL_SV + SWA_KV_W
PROJ_TN = 1024
J_SQ = COL_SQ // PROJ_TN
J_SKV = COL_SK // PROJ_TN

MIB = 1024 * 1024


def _cparams(semantics, vmem_mib):
    return pltpu.CompilerParams(dimension_semantics=semantics, vmem_limit_bytes=vmem_mib * MIB)


def _rms(xf, g):
    y = xf * lax.rsqrt(jnp.mean(xf * xf, axis=-1, keepdims=True) + EPS)
    return y * g


def _rope(a, cos_ref, sin_ref):
    width = a.shape[1]
    nh = width // SWA_HEAD_DIM
    cos = jnp.concatenate([cos_ref[...]] * nh, axis=1)
    sin = jnp.concatenate([sin_ref[...]] * nh, axis=1)
    lane = lax.broadcasted_iota(jnp.int32, a.shape, 1) & (SWA_HEAD_DIM - 1)
    partner = jnp.where(lane < ROPE_HALF, pltpu.roll(a, width - ROPE_HALF, 1), pltpu.roll(a, ROPE_HALF, 1))
    return a * cos + partner * sin


def _inproj_kernel(x_ref, g_ref, w_ref, wlr_ref, cos_ref, sin_ref, proj_ref, lr_ref, h_ref):
    j = pl.program_id(1)

    @pl.when(j == 0)
    def _():
        h = _rms(x_ref[...], g_ref[...]).astype(BF16)
        h_ref[...] = h
        lr_ref[...] = jnp.dot(h, wlr_ref[...], preferred_element_type=F32).astype(BF16)

    acc = jnp.dot(h_ref[...], w_ref[...], preferred_element_type=F32)

    @pl.when(j == J_SQ)
    def _():
        proj_ref[...] = _rope(acc, cos_ref, sin_ref).astype(BF16)

    @pl.when(j == J_SKV)
    def _():
        roped = _rope(acc[:, :SWA_KV_W], cos_ref, sin_ref)
        proj_ref[...] = jnp.concatenate([roped, acc[:, SWA_KV_W:]], axis=1).astype(BF16)

    @pl.when(jnp.logical_and(j != J_SQ, j != J_SKV))
    def _():
        proj_ref[...] = acc.astype(BF16)


def _in_projection(x2d, seq, g, w_main, w_lr, cos_t, sin_t, tm):
    t = x2d.shape[0]
    pos_blocks = seq // tm
    return pl.pallas_call(
        _inproj_kernel,
        grid=(t // tm, PROJ_COLS // PROJ_TN),
        in_specs=[
            pl.BlockSpec((tm, D_MODEL), lambda i, j: (i, 0)),
            pl.BlockSpec((1, D_MODEL), lambda i, j: (0, 0)),
            pl.BlockSpec((D_MODEL, PROJ_TN), lambda i, j: (0, j)),
            pl.BlockSpec((D_MODEL, LANES), lambda i, j: (0, 0)),
            pl.BlockSpec((tm, SWA_HEAD_DIM), lambda i, j: (i % pos_blocks, 0)),
            pl.BlockSpec((tm, SWA_HEAD_DIM), lambda i, j: (i % pos_blocks, 0)),
        ],
        out_specs=[
            pl.BlockSpec((tm, PROJ_TN), lambda i, j: (i, j)),
            pl.BlockSpec((tm, LANES), lambda i, j: (i, 0)),
        ],
        out_shape=[
            jax.ShapeDtypeStruct((t, PROJ_COLS), BF16),
            jax.ShapeDtypeStruct((t, LANES), BF16),
        ],
        scratch_shapes=[pltpu.VMEM((tm, D_MODEL), BF16)],
        compiler_params=_cparams(("parallel", "arbitrary"), 52),
        name="in_projection",
    )(x2d, g, w_main, w_lr, cos_t, sin_t)


def _gla_kernel(q_ref, k_ref, v_ref, r_ref, lr_ref, wdec_ref, bdec_ref, gout_ref, o_ref,
                oacc_ref, state_ref, b_ref):
    seq = q_ref.shape[1]
    heads = q_ref.shape[2] // GLA_DK
    c = GLA_CHUNK
    n_chunks = seq // c
    half = n_chunks // 2
    scale = GLA_DK ** -0.5

    state_ref[...] = jnp.zeros_like(state_ref)

    row = lax.broadcasted_iota(jnp.int32, (c, c), 0)
    col = lax.broadcasted_iota(jnp.int32, (c, c), 1)
    cum_mat = (jnp.where(col <= row, 1.0, 0.0).astype(BF16), jnp.where(col >= row, 1.0, 0.0).astype(BF16))
    att_mask = (col <= row, col > row)

    kw = heads * GLA_DK

    def decay_block(j, carry):
        r0 = pl.multiple_of(j * GLA_DECAY_BLOCK, GLA_DECAY_BLOCK)
        lr = lr_ref[0, pl.ds(r0, GLA_DECAY_BLOCK), :]
        zs = [jnp.dot(lr, wdec_ref[d], preferred_element_type=F32) + bdec_ref[d] for d in range(2)]
        for d in range(2):
            log_a = (jnp.minimum(zs[d], 0.0) - jnp.log(1.0 + jnp.exp(-jnp.abs(zs[d])))) / GLA_GATE_NORM
            la_hi = log_a.astype(BF16)
            la_lo = (log_a - la_hi.astype(F32)).astype(BF16)
            for m in range(GLA_DECAY_BLOCK // c):
                sl = slice(m * c, (m + 1) * c)
                b = (jnp.dot(cum_mat[d], la_hi[sl], preferred_element_type=F32)
                     + jnp.dot(cum_mat[d], la_lo[sl], preferred_element_type=F32))
                b_ref[pl.ds(r0 + m * c, c), d * kw:(d + 1) * kw] = b
        return carry

    lax.fori_loop(0, seq // GLA_DECAY_BLOCK, decay_block, 0)

    def advance(i):
        rows = [pl.ds(pl.multiple_of(n * c, c), c) for n in (i, n_chunks - 1 - i)]
        work = []
        for d in range(2):
            b = b_ref[rows[d], d * kw:(d + 1) * kw]
            b_edge = b[c - 1:c, :] if d == 0 else b[0:1, :]
            q = q_ref[0, rows[d], :].astype(F32) * scale
            k = k_ref[0, rows[d], :].astype(F32)
            q_dec = (q * jnp.exp(b)).astype(BF16)
            k_dec = (k * jnp.exp(-b)).astype(BF16)
            k_end = (k * jnp.exp(b_edge - b)).astype(BF16)
            decay = jnp.exp(b_edge)
            for hh in range(heads):
                kc = slice(hh * GLA_DK, (hh + 1) * GLA_DK)
                att = lax.dot_general(q_dec[:, kc], k_dec[:, kc], (((1,), (1,)), ((), ())),
                                      preferred_element_type=F32)
                att = jnp.where(att_mask[d], att, 0.0).astype(BF16)
                work.append((d, hh, q_dec[:, kc], k_end[:, kc], decay[:, kc], att))
        outs = []
        for d, hh, q_dec, k_end, decay, att in work:
            vcols = slice(hh * GLA_DV, (hh + 1) * GLA_DV)
            v = v_ref[0, rows[d], vcols]
            slot = d * heads + hh
            state = state_ref[slot]
            o = jnp.dot(att, v, preferred_element_type=F32)
            o = o + lax.dot_general(q_dec, state.astype(BF16), (((1,), (1,)), ((), ())),
                                    preferred_element_type=F32)
            kv = lax.dot_general(v, k_end, (((0,), (0,)), ((), ())), preferred_element_type=F32)
            state_ref[slot] = decay * state + kv
            outs.append((rows[d], vcols, o))
        return outs

    def finalize(rows, vcols, o):
        o = o * lax.rsqrt(jnp.mean(o * o, axis=-1, keepdims=True) + EPS) * gout_ref[:, vcols]
        r = r_ref[0, rows, vcols].astype(F32)
        o_ref[0, rows, vcols] = (o * (r * jax.nn.sigmoid(r))).astype(BF16)

    def first_half(i, carry):
        for rows, vcols, o in advance(i):
            oacc_ref[rows, vcols] = o
        return carry

    def second_half(i, carry):
        for rows, vcols, o in advance(i):
            finalize(rows, vcols, oacc_ref[rows, vcols] + o)
        return carry

    lax.fori_loop(0, half, first_half, 0)
    lax.fori_loop(half, n_chunks, second_half, 0)


def _gla(proj3, lr3, wdec_pad, bdec, gout, heads_per_step):
    bsz, seq, _ = proj3.shape
    hp = heads_per_step
    kw, vw = hp * GLA_DK, hp * GLA_DV
    qb, kb = COL_GQ // kw, COL_GK // kw
    vb, rb = COL_GV // vw, COL_GR // vw
    return pl.pallas_call(
        _gla_kernel,
        grid=(bsz, GLA_HEADS // hp),
        in_specs=[
            pl.BlockSpec((1, seq, kw), lambda b, h: (b, 0, qb + h)),
            pl.BlockSpec((1, seq, kw), lambda b, h: (b, 0, kb + h)),
            pl.BlockSpec((1, seq, vw), lambda b, h: (b, 0, vb + h)),
            pl.BlockSpec((1, seq, vw), lambda b, h: (b, 0, rb + h)),
            pl.BlockSpec((1, seq, LANES), lambda b, h: (b, 0, 0)),
            pl.BlockSpec((2, LANES, kw), lambda b, h: (0, 0, h)),
            pl.BlockSpec((2, 1, kw), lambda b, h: (0, 0, h)),
            pl.BlockSpec((1, vw), lambda b, h: (0, h)),
        ],
        out_specs=pl.BlockSpec((1, seq, vw), lambda b, h: (b, 0, h)),
        out_shape=jax.ShapeDtypeStruct((bsz, seq, GLA_V_W), BF16),
        scratch_shapes=[
            pltpu.VMEM((seq, vw), F32),
            pltpu.VMEM((2 * hp, GLA_DV, GLA_DK), F32),
            pltpu.VMEM((seq, 2 * kw), F32),
        ],
        compiler_params=_cparams(("parallel", "parallel"), 56),
        name="gla",
    )(proj3, proj3, proj3, proj3, lr3, wdec_pad, bdec, gout)


def _swa_kernel(sink_ref, q_ref, kp_ref, kc_ref, kn_ref, vp_ref, vc_ref, vn_ref, o_ref, kbuf_ref, vbuf_ref):
    t = pl.program_id(1)
    tq = q_ref.shape[1]
    w = SWA_WINDOW
    hd = SWA_HEAD_DIM
    groups = SWA_HEADS // SWA_KV_HEADS
    seq = pl.num_programs(1) * tq
    scale = hd ** -0.5

    kbuf_ref[0:w, :] = kp_ref[0]
    kbuf_ref[w:w + tq, :] = kc_ref[0]
    kbuf_ref[w + tq:, :] = kn_ref[0]
    vbuf_ref[0:w, :] = vp_ref[0]
    vbuf_ref[w:w + tq, :] = vc_ref[0]
    vbuf_ref[w + tq:, :] = vn_ref[0]

    qi = lax.broadcasted_iota(jnp.int32, (groups * w, 3 * w), 0) & (w - 1)
    kj = lax.broadcasted_iota(jnp.int32, (groups * w, 3 * w), 1)
    in_window = jnp.abs(kj - w - qi) <= w

    def block(m, carry):
        q0 = pl.multiple_of(m * w, w)
        kpos = t * tq + q0 - w + kj
        valid = in_window & (kpos >= 0) & (kpos < seq)
        row_g = lax.broadcasted_iota(jnp.int32, (groups * w, 1), 0) // w
        scores = []
        for kh in range(SWA_KV_HEADS):
            qs = jnp.concatenate(
                [q_ref[0, pl.ds(q0, w), (kh * groups + g) * hd:(kh * groups + g + 1) * hd] for g in range(groups)],
                axis=0)
            ks = kbuf_ref[pl.ds(q0, 3 * w), kh * hd:(kh + 1) * hd]
            scores.append(lax.dot_general(qs, ks, (((1,), (1,)), ((), ())), preferred_element_type=F32))
        probs = []
        for kh in range(SWA_KV_HEADS):
            s = jnp.where(valid, scores[kh] * scale, -jnp.inf)
            sink = jnp.zeros((groups * w, 1), F32)
            for g in range(groups):
                sink = jnp.where(row_g == g, sink_ref[kh * groups + g], sink)
            mx = jnp.maximum(jnp.max(s, axis=-1, keepdims=True), sink)
            p = jnp.exp(s - mx)
            denom = jnp.sum(p, axis=-1, keepdims=True) + jnp.exp(sink - mx)
            probs.append((p.astype(BF16), denom))
        for kh in range(SWA_KV_HEADS):
            p, denom = probs[kh]
            vs = vbuf_ref[pl.ds(q0, 3 * w), kh * hd:(kh + 1) * hd]
            o = jnp.dot(p, vs, preferred_element_type=F32) / denom
            for g in range(groups):
                head = kh * groups + g
                o_ref[0, pl.ds(q0, w), head * hd:(head + 1) * hd] = o[g * w:(g + 1) * w, :].astype(BF16)
        return carry

    lax.fori_loop(0, tq // w, block, 0)


def _swa(proj3, sink, tq):
    bsz, seq, _ = proj3.shape
    w = SWA_WINDOW
    r = tq // w
    last = seq // w - 1
    qb = COL_SQ // SWA_Q_W
    kb, vb = COL_SK // SWA_KV_W, COL_SV // SWA_KV_W

    def neighbours(cb):
        return [
            pl.BlockSpec((1, w, SWA_KV_W), lambda b, t: (b, jnp.maximum(t * r - 1, 0), cb)),
            pl.BlockSpec((1, tq, SWA_KV_W), lambda b, t: (b, t, cb)),
            pl.BlockSpec((1, w, SWA_KV_W), lambda b, t: (b, jnp.minimum((t + 1) * r, last), cb)),
        ]

    return pl.pallas_call(
        _swa_kernel,
        grid=(bsz, seq // tq),
        in_specs=[pl.BlockSpec(memory_space=pltpu.SMEM),
                  pl.BlockSpec((1, tq, SWA_Q_W), lambda b, t: (b, t, qb))] + neighbours(kb) + neighbours(vb),
        out_specs=pl.BlockSpec((1, tq, SWA_Q_W), lambda b, t: (b, t, 0)),
        out_shape=jax.ShapeDtypeStruct((bsz, seq, SWA_Q_W), BF16),
        scratch_shapes=[pltpu.VMEM((tq + 2 * w, SWA_KV_W), BF16), pltpu.VMEM((tq + 2 * w, SWA_KV_W), BF16)],
        compiler_params=_cparams(("parallel", "parallel"), 32),
        name="swa",
    )(sink, proj3, proj3, proj3, proj3, proj3, proj3, proj3)


def _merge_kernel(x_ref, og_ref, os_ref, g0_ref, g1_ref, bm_ref, wg_ref, ws_ref, wo_ref, o_ref):
    a = jnp.dot(og_ref[...], wg_ref[...], preferred_element_type=F32)
    b = jnp.dot(os_ref[...], ws_ref[...], preferred_element_type=F32)
    g0 = jax.nn.sigmoid(g0_ref[...].astype(F32) + bm_ref[0:1, :])
    g1 = jax.nn.sigmoid(g1_ref[...].astype(F32) + bm_ref[1:2, :])
    merged = (g0 * a + g1 * b).astype(BF16)
    o_ref[...] = x_ref[...] + jnp.dot(merged, wo_ref[...], preferred_element_type=F32)


def _resident(shape):
    return pl.BlockSpec(shape, lambda *_: (0,) * len(shape), pipeline_mode=pl.Buffered(1))


def _merge(x2d, o_gla, o_swa, proj, b_merge, w_pg, w_ps, w_out, tm):
    t = x2d.shape[0]
    return pl.pallas_call(
        _merge_kernel,
        grid=(t // tm,),
        in_specs=[
            pl.BlockSpec((tm, D_MODEL), lambda i: (i, 0)),
            pl.BlockSpec((tm, GLA_V_W), lambda i: (i, 0)),
            pl.BlockSpec((tm, SWA_Q_W), lambda i: (i, 0)),
            pl.BlockSpec((tm, D_MODEL), lambda i: (i, 0)),
            pl.BlockSpec((tm, D_MODEL), lambda i: (i, 1)),
            _resident((2, D_MODEL)),
            _resident((GLA_V_W, D_MODEL)),
            _resident((SWA_Q_W, D_MODEL)),
            _resident((D_MODEL, D_MODEL)),
        ],
        out_specs=pl.BlockSpec((tm, D_MODEL), lambda i: (i, 0)),
        out_shape=jax.ShapeDtypeStruct((t, D_MODEL), F32),
        compiler_params=_cparams(("parallel",), 56),
        name="merge",
    )(x2d, o_gla, o_swa, proj, proj, b_merge, w_pg, w_ps, w_out)


def _norm_matmul_kernel(x_ref, g_ref, w_ref, o_ref):
    h = _rms(x_ref[...], g_ref[...]).astype(BF16)
    o_ref[...] = jnp.dot(h, w_ref[...], preferred_element_type=F32).astype(o_ref.dtype)


def _mem_kv(mem2d, g, w_kv, tm):
    t = mem2d.shape[0]
    n = w_kv.shape[1]
    return pl.pallas_call(
        _norm_matmul_kernel,
        grid=(t // tm,),
        in_specs=[
            pl.BlockSpec((tm, D_MODEL), lambda i: (i, 0)),
            _resident((1, D_MODEL)),
            _resident((D_MODEL, n)),
        ],
        out_specs=pl.BlockSpec((tm, n), lambda i: (i, 0)),
        out_shape=jax.ShapeDtypeStruct((t, n), BF16),
        compiler_params=_cparams(("parallel",), 32),
        name="mem_kv",
    )(mem2d, g, w_kv)


def _cross_kernel(x_ref, g_ref, kv_ref, wq_ref, wo_ref, o_ref):
    x = x_ref[0]
    h = _rms(x, g_ref[...]).astype(BF16)
    q = jnp.dot(h, wq_ref[...], preferred_element_type=F32).astype(BF16)
    hd = CROSS_HEAD_DIM
    scale = hd ** -0.5
    scores = [lax.dot_general(q[:, head * hd:(head + 1) * hd], kv_ref[0, :, head * hd:(head + 1) * hd],
                              (((1,), (1,)), ((), ())), preferred_element_type=F32)
              for head in range(CROSS_HEADS)]
    probs = []
    for s in scores:
        s = s * scale
        p = jnp.exp(s - jnp.max(s, axis=-1, keepdims=True))
        probs.append((p.astype(BF16), jnp.sum(p, axis=-1, keepdims=True)))
    outs = []
    for head, (p, denom) in enumerate(probs):
        vh = kv_ref[0, :, CROSS_W + head * hd:CROSS_W + (head + 1) * hd]
        outs.append((jnp.dot(p, vh, preferred_element_type=F32) / denom).astype(BF16))
    o = jnp.concatenate(outs, axis=1)
    o_ref[0] = x + jnp.dot(o, wo_ref[...], preferred_element_type=F32)


def _cross(x3, kv3, g, w_q, w_o, tm):
    bsz, seq, _ = x3.shape
    return pl.pallas_call(
        _cross_kernel,
        grid=(bsz, seq // tm),
        in_specs=[
            pl.BlockSpec((1, tm, D_MODEL), lambda b, i: (b, i, 0)),
            _resident((1, D_MODEL)),
            pl.BlockSpec((1, N_MEM, 2 * CROSS_W), lambda b, i: (b, 0, 0)),
            _resident((D_MODEL, CROSS_W)),
            _resident((CROSS_W, D_MODEL)),
        ],
        out_specs=pl.BlockSpec((1, tm, D_MODEL), lambda b, i: (b, i, 0)),
        out_shape=jax.ShapeDtypeStruct((bsz, seq, D_MODEL), F32),
        compiler_params=_cparams(("parallel", "parallel"), 48),
        name="cross_attention",
    )(x3, g, kv3, w_q, w_o)


HALF_D = D_MODEL // 2
COMBINE_TM = 256
ROUTE_TM = COMBINE_TM
DISPATCH_TM = 512
EXPERT_ROWS = 512


def _pack_rows(a):
    return pltpu.pack_elementwise([a[:, :HALF_D], a[:, HALF_D:]], packed_dtype=BF16)


def _unpack_rows(w):
    lo = pltpu.unpack_elementwise(w, index=0, packed_dtype=BF16, unpacked_dtype=F32)
    hi = pltpu.unpack_elementwise(w, index=1, packed_dtype=BF16, unpacked_dtype=F32)
    return jnp.concatenate([lo, hi], axis=1)


def _route_kernel(x_ref, g_ref, wr_ref, br_ref, cnt0_ref, hp_ref, er_ref, within_ref, comb_ref, cnt_ref,
                  first_ref, tcnt_ref, run_ref):
    i = pl.program_id(0)
    tm = x_ref.shape[0]
    lane = lax.broadcasted_iota(jnp.int32, (tm, LANES), 1)

    @pl.when(i == 0)
    def _():
        run_ref[...] = cnt0_ref[...]

    hf = _rms(x_ref[...], g_ref[...])
    h_hi = hf.astype(BF16)
    h_lo = (hf - h_hi.astype(F32)).astype(BF16)
    logits = (jnp.dot(h_hi, wr_ref[0], preferred_element_type=F32)
              + jnp.dot(h_lo, wr_ref[0], preferred_element_type=F32)
              + jnp.dot(h_hi, wr_ref[1], preferred_element_type=F32)) + br_ref[...]
    logits = jnp.where(lane < N_EXPERTS, logits, -jnp.inf)
    vals, hots, idxs = [], [], []
    for _ in range(TOP_K):
        mx = jnp.max(logits, axis=-1, keepdims=True)
        idx = jnp.min(jnp.where(logits == mx, lane, LANES), axis=-1, keepdims=True)
        hot = lane == idx
        vals.append(mx)
        hots.append(hot)
        idxs.append(idx)
        logits = jnp.where(hot, -jnp.inf, logits)
    exps = [jnp.exp(v - vals[0]) for v in vals]
    denom = exps[0] + exps[1] + exps[2] + exps[3]
    sel = jnp.zeros((tm, LANES), F32)
    comb = jnp.zeros((tm, LANES), F32)
    for k in range(TOP_K):
        sel = sel + jnp.where(hots[k], 1.0, 0.0)
        comb = comb + jnp.where(hots[k], exps[k] / denom, 0.0)
    comb_ref[...] = comb

    row = lax.broadcasted_iota(jnp.int32, (tm, tm), 0)
    col = lax.broadcasted_iota(jnp.int32, (tm, tm), 1)
    earlier = jnp.where(col < row, 1.0, 0.0).astype(BF16)
    within = jnp.dot(earlier, sel.astype(BF16), preferred_element_type=F32)
    rank = within + run_ref[...]
    tile_count = jnp.sum(sel, axis=0, keepdims=True)
    within_ref[...] = within
    first_ref[0] = run_ref[...]
    tcnt_ref[0] = tile_count
    run_ref[...] = run_ref[...] + tile_count
    cnt_ref[...] = run_ref[...]

    table = jnp.zeros((tm, LANES), jnp.int32)
    for k in range(TOP_K):
        rank_k = jnp.sum(jnp.where(hots[k], rank, 0.0), axis=-1, keepdims=True).astype(jnp.int32)
        table = jnp.where(lane == k, idxs[k], table)
        table = jnp.where(lane == TOP_K + k, rank_k, table)
    er_ref[...] = jnp.transpose(table)[:2 * TOP_K, :]
    hp_ref[...] = _pack_rows(hf)


def _route(x2d, g, w_router, b_router, cnt0):
    t = x2d.shape[0]
    tm = ROUTE_TM
    const = lambda *shape: pl.BlockSpec(shape, lambda i: (0,) * len(shape))
    return pl.pallas_call(
        _route_kernel,
        grid=(t // tm,),
        in_specs=[
            pl.BlockSpec((tm, D_MODEL), lambda i: (i, 0)),
            const(1, D_MODEL),
            const(2, D_MODEL, LANES),
            const(1, LANES),
            const(1, LANES),
        ],
        out_specs=[
            pl.BlockSpec((tm, HALF_D), lambda i: (i, 0)),
            pl.BlockSpec((2 * TOP_K, tm), lambda i: (0, i)),
            pl.BlockSpec((tm, LANES), lambda i: (i, 0)),
            pl.BlockSpec((tm, LANES), lambda i: (i, 0)),
            const(1, LANES),
            pl.BlockSpec((1, 1, LANES), lambda i: (i, 0, 0)),
            pl.BlockSpec((1, 1, LANES), lambda i: (i, 0, 0)),
        ],
        out_shape=[
            jax.ShapeDtypeStruct((t, HALF_D), jnp.uint32),
            jax.ShapeDtypeStruct((2 * TOP_K, t), jnp.int32),
            jax.ShapeDtypeStruct((t, LANES), F32),
            jax.ShapeDtypeStruct((t, LANES), F32),
            jax.ShapeDtypeStruct((1, LANES), F32),
            jax.ShapeDtypeStruct((t // tm, 1, LANES), F32),
            jax.ShapeDtypeStruct((t // tm, 1, LANES), F32),
        ],
        scratch_shapes=[pltpu.VMEM((1, LANES), F32)],
        compiler_params=_cparams(("arbitrary",), 40),
        name="moe_route",
    )(x2d, g, w_router, b_router, cnt0)


ROW_DMA_UNROLL = 4


def _dispatch_kernel(tv_ref, pos_ref, hp_ref, hs_ref, zero_ref, sem, zero_sem):
    tm = hp_ref.shape[0]

    @pl.when(pl.program_id(0) == 0)
    def _():
        zero_ref[...] = _pack_rows(jnp.zeros((zero_ref.shape[0], D_MODEL), F32))

        def tile_fill(g):
            r0 = pl.multiple_of(g * EXPERT_ROWS, EXPERT_ROWS)
            return pltpu.make_async_copy(zero_ref, hs_ref.at[pl.ds(r0, EXPERT_ROWS), :], zero_sem)

        def start_fill(g, carry):
            @pl.when(tv_ref[g] < EXPERT_ROWS)
            def _():
                tile_fill(g).start()
            return carry

        def wait_fill(g, carry):
            @pl.when(tv_ref[g] < EXPERT_ROWS)
            def _():
                tile_fill(g).wait()
            return carry

        n_tiles = hs_ref.shape[0] // EXPERT_ROWS
        lax.fori_loop(0, n_tiles, start_fill, 0)
        lax.fori_loop(0, n_tiles, wait_fill, 0)

    def row_copy(t, k):
        return pltpu.make_async_copy(hp_ref.at[pl.ds(t, 1), :], hs_ref.at[pl.ds(pos_ref[k, t], 1), :], sem)

    def issue(t, carry):
        for k in range(TOP_K):
            row_copy(t, k).start()
        return carry

    def drain(t, carry):
        for k in range(TOP_K):
            row_copy(t, k).wait()
        return carry

    lax.fori_loop(0, tm, issue, 0, unroll=ROW_DMA_UNROLL)
    lax.fori_loop(0, tm, drain, 0, unroll=ROW_DMA_UNROLL)


def _dispatch(tile_valid, pos, hp, n_rows):
    t = hp.shape[0]
    tm = DISPATCH_TM
    return pl.pallas_call(
        _dispatch_kernel,
        grid=(t // tm,),
        in_specs=[
            pl.BlockSpec(memory_space=pltpu.SMEM),
            pl.BlockSpec((TOP_K, tm), lambda i: (0, i), memory_space=pltpu.SMEM),
            pl.BlockSpec((tm, HALF_D), lambda i: (i, 0)),
        ],
        out_specs=pl.BlockSpec(memory_space=pl.ANY),
        out_shape=jax.ShapeDtypeStruct((n_rows, HALF_D), jnp.uint32),
        scratch_shapes=[pltpu.VMEM((EXPERT_ROWS, HALF_D), jnp.uint32), pltpu.SemaphoreType.DMA,
                        pltpu.SemaphoreType.DMA],
        compiler_params=_cparams(("arbitrary",), 32),
        name="moe_dispatch",
    )(tile_valid, pos, hp)


def _expert_kernel(te_ref, tv_ref, tb_ref, hs_ref, wup_ref, bup_ref, wdn_ref, ys_ref):
    g = pl.program_id(0)
    valid = tv_ref[g]

    @pl.when(valid == 0)
    def _():
        ys_ref[...] = jnp.zeros_like(ys_ref)

    @pl.when(valid > 0)
    def _():
        x = _unpack_rows(hs_ref[...]).astype(BF16)
        up = jnp.dot(x, wup_ref[0], preferred_element_type=F32) + bup_ref[0]
        glu = jnp.minimum(up[:, :D_FF], SWIGLU_LIMIT)
        lin = jnp.clip(up[:, D_FF:], -SWIGLU_LIMIT, SWIGLU_LIMIT)
        act = glu * jax.nn.sigmoid(SWIGLU_ALPHA * glu) * (lin + 1.0)
        y = jnp.dot(act.astype(BF16), wdn_ref[0], preferred_element_type=F32)
        ys_ref[...] = y.astype(BF16)


def _experts(tile_expert, tile_valid, tile_block, hs, w_up, b_up, w_down):
    n_rows = hs.shape[0]
    rows = EXPERT_ROWS
    grid_spec = pltpu.PrefetchScalarGridSpec(
        num_scalar_prefetch=3,
        grid=(n_rows // rows,),
        in_specs=[
            pl.BlockSpec((rows, HALF_D), lambda g, te, tv, tb: (tb[g], 0)),
            pl.BlockSpec((1, D_MODEL, 2 * D_FF), lambda g, te, tv, tb: (te[g], 0, 0)),
            pl.BlockSpec((1, 1, 2 * D_FF), lambda g, te, tv, tb: (te[g], 0, 0)),
            pl.BlockSpec((1, D_FF, D_MODEL), lambda g, te, tv, tb: (te[g], 0, 0)),
        ],
        out_specs=pl.BlockSpec((rows, D_MODEL), lambda g, te, tv, tb: (g, 0)),
    )
    return pl.pallas_call(
        _expert_kernel,
        grid_spec=grid_spec,
        out_shape=jax.ShapeDtypeStruct((n_rows, D_MODEL), BF16),
        compiler_params=_cparams(("arbitrary",), 32),
        name="moe_experts",
    )(tile_expert, tile_valid, tile_block, hs, w_up, b_up, w_down)


COMBINE_WIN = 64
ROW_ALIGN = 16


def _combine_kernel(ws_ref, nx_ref, x_ref, within_ref, comb_ref, shift_ref, expand_ref, bdn_ref, gf_ref, ys_ref,
                    o_ref, win0_ref, win1_ref, extra_ref, acc_ref, sem0, sem1, extra_sem, *, tile_offset):
    i = pl.program_id(0)
    last = pl.num_programs(0) - 1
    tm = x_ref.shape[0]
    w = COMBINE_WIN
    tile = tile_offset + i
    next_tile = tile_offset + jnp.minimum(i + 1, last)

    def step(own_buf, own_sem, spare_buf, spare_sem):
        def window(t, e, buf, sem):
            r0 = pl.multiple_of(ws_ref[t * N_EXPERTS + e], ROW_ALIGN)
            return pltpu.make_async_copy(ys_ref.at[pl.ds(r0, w), :], buf.at[pl.ds(e * w, w), :], sem)

        @pl.when(i == 0)
        def _():
            for e in range(N_EXPERTS):
                window(tile, e, own_buf, own_sem).start()

        for e in range(N_EXPERTS):
            window(next_tile, e, spare_buf, spare_sem).start()

        rel = within_ref[...] + shift_ref[0]
        comb = comb_ref[...]
        rel_b = jnp.dot(rel.astype(BF16), expand_ref[...], preferred_element_type=F32)
        gate_b = jnp.dot(comb.astype(BF16), expand_ref[...], preferred_element_type=F32)
        slot = (lax.broadcasted_iota(jnp.int32, rel_b.shape, 1) & (w - 1)).astype(F32)
        place = jnp.where(rel_b == slot, gate_b, 0.0).astype(BF16)
        acc = x_ref[...] + jnp.dot(comb.astype(BF16), bdn_ref[...], preferred_element_type=F32)

        for e in range(N_EXPERTS):
            window(tile, e, own_buf, own_sem).wait()
        acc_ref[...] = acc + jnp.dot(place, own_buf[...], preferred_element_type=F32)

        def overflow(e, carry):
            more = nx_ref[tile * N_EXPERTS + e]

            @pl.when(more > 0)
            def _():
                lane = lax.broadcasted_iota(jnp.int32, (tm, LANES), 1)
                rel_e = jnp.sum(jnp.where(lane == e, within_ref[...] + shift_ref[0], 0.0), axis=-1, keepdims=True)
                gate_e = jnp.sum(jnp.where(lane == e, comb_ref[...], 0.0), axis=-1, keepdims=True)
                r0 = pl.multiple_of(ws_ref[tile * N_EXPERTS + e], ROW_ALIGN)
                cols = lax.broadcasted_iota(jnp.int32, (tm, w), 1).astype(F32)

                def further(n, inner):
                    cp = pltpu.make_async_copy(ys_ref.at[pl.ds(r0 + n * w, w), :], extra_ref, extra_sem)
                    cp.start()
                    cp.wait()
                    hit = jnp.where(rel_e - (n * w).astype(F32) == cols, gate_e, 0.0).astype(BF16)
                    acc_ref[...] += jnp.dot(hit, extra_ref[...], preferred_element_type=F32)
                    return inner

                lax.fori_loop(1, more + 1, further, 0)

            return carry

        lax.fori_loop(0, N_EXPERTS, overflow, 0)

        o_ref[...] = _rms(acc_ref[...], gf_ref[...])

        @pl.when(i == last)
        def _():
            for e in range(N_EXPERTS):
                window(next_tile, e, spare_buf, spare_sem).wait()

    @pl.when(lax.rem(i, 2) == 0)
    def _():
        step(win0_ref, sem0, win1_ref, sem1)

    @pl.when(lax.rem(i, 2) == 1)
    def _():
        step(win1_ref, sem1, win0_ref, sem0)


def _combine(win_start, win_more, win_shift, tile_offset, x2d, within, comb, expand, b_down, g_final, ys):
    t = x2d.shape[0]
    tm = COMBINE_TM
    const = lambda *shape: pl.BlockSpec(shape, lambda i: (0,) * len(shape))
    smem = pl.BlockSpec(memory_space=pltpu.SMEM)
    return pl.pallas_call(
        functools.partial(_combine_kernel, tile_offset=tile_offset),
        grid=(t // tm,),
        in_specs=[
            smem,
            smem,
            pl.BlockSpec((tm, D_MODEL), lambda i: (i, 0)),
            pl.BlockSpec((tm, LANES), lambda i: (i, 0)),
            pl.BlockSpec((tm, LANES), lambda i: (i, 0)),
            pl.BlockSpec((1, 1, LANES), lambda i: (tile_offset + i, 0, 0)),
            const(LANES, N_EXPERTS * COMBINE_WIN),
            const(LANES, D_MODEL),
            const(1, D_MODEL),
            pl.BlockSpec(memory_space=pl.ANY),
        ],
        out_specs=pl.BlockSpec((tm, D_MODEL), lambda i: (i, 0)),
        out_shape=jax.ShapeDtypeStruct((t, D_MODEL), F32),
        scratch_shapes=[pltpu.VMEM((N_EXPERTS * COMBINE_WIN, D_MODEL), BF16),
                        pltpu.VMEM((N_EXPERTS * COMBINE_WIN, D_MODEL), BF16),
                        pltpu.VMEM((COMBINE_WIN, D_MODEL), BF16),
                        pltpu.VMEM((tm, D_MODEL), F32),
                        pltpu.SemaphoreType.DMA, pltpu.SemaphoreType.DMA, pltpu.SemaphoreType.DMA],
        compiler_params=_cparams(("arbitrary",), 48),
        name="moe_combine",
    )(win_start, win_more, x2d, within, comb, win_shift, expand, b_down, g_final, ys)


def _tile_plan(counts, n_tiles):
    rows = EXPERT_ROWS
    tiles_e = (counts + rows - 1) // rows
    tile_end = jnp.cumsum(tiles_e)
    tile_start = tile_end - tiles_e
    total = tile_end[-1]
    g = jnp.arange(n_tiles, dtype=jnp.int32)
    last = total - 1
    g_eff = jnp.minimum(g, last)
    te = jnp.sum((tile_end[None, :] <= g_eff[:, None]).astype(jnp.int32), axis=1)
    te = jnp.minimum(te, N_EXPERTS - 1)
    valid = jnp.clip(counts[te] - (g_eff - tile_start[te]) * rows, 0, rows)
    valid = jnp.where(g < total, valid, 0).astype(jnp.int32)
    return (tile_start * rows).astype(jnp.int32), te, valid, g_eff.astype(jnp.int32)


def _moe(x2_groups, p):
    n_tok = sum(x.shape[0] for x in x2_groups)
    n_tiles = n_tok * TOP_K // EXPERT_ROWS + N_EXPERTS + 1
    cnt = jnp.zeros((1, LANES), F32)
    routed = []
    for x2d in x2_groups:
        hp, er, within, comb, cnt, first, tile_cnt = _route(x2d, p["norm_moe"], p["w_router"], p["b_router"], cnt)
        routed.append((hp, er, within, comb, first, tile_cnt))
    counts = cnt[0, :N_EXPERTS].astype(jnp.int32)
    group_start, tile_expert, tile_valid, tile_block = _tile_plan(counts, n_tiles)
    hp_all = jnp.concatenate([r[0] for r in routed], axis=0)
    er_all = jnp.concatenate([r[1] for r in routed], axis=1)
    first = jnp.concatenate([r[4] for r in routed], axis=0)[:, 0, :N_EXPERTS].astype(jnp.int32)
    tile_cnt = jnp.concatenate([r[5] for r in routed], axis=0)[:, 0, :N_EXPERTS].astype(jnp.int32)
    seg_start = group_start[None, :] + first
    win_start = (seg_start // ROW_ALIGN) * ROW_ALIGN
    shift = seg_start - win_start
    win_more = jnp.maximum((shift + tile_cnt + COMBINE_WIN - 1) // COMBINE_WIN - 1, 0)
    win_shift = jnp.pad(shift.astype(F32), ((0, 0), (0, LANES - N_EXPERTS)))[:, None, :]
    lane_expert = jnp.arange(N_EXPERTS * COMBINE_WIN, dtype=jnp.int32) // COMBINE_WIN
    expand = (jnp.arange(LANES, dtype=jnp.int32)[:, None] == lane_expert[None, :]).astype(BF16)
    experts = jnp.arange(N_EXPERTS, dtype=jnp.int32)[:, None, None]
    pos = jnp.sum(jnp.where(er_all[None, :TOP_K] == experts, group_start[:, None, None], 0), axis=0) + er_all[TOP_K:]
    hs = _dispatch(tile_valid, pos, hp_all, n_tiles * EXPERT_ROWS)
    ys = _experts(tile_expert, tile_valid, tile_block, hs, p["w_up"], p["b_up"], p["w_down"])
    outs, tile_offset = [], 0
    for x2d, (_, _, within, comb, _, _) in zip(x2_groups, routed):
        outs.append(_combine(win_start.reshape(-1), win_more.reshape(-1), win_shift, tile_offset, x2d, within, comb,
                             expand, p["b_down"], p["norm_final"], ys))
        tile_offset += x2d.shape[0] // COMBINE_TM
    return outs


def _deinterleave_kernel(w_ref, perm_ref, o_ref):
    o_ref[0] = jnp.dot(w_ref[0].astype(BF16), perm_ref[...], preferred_element_type=F32).astype(BF16)


def _deinterleave_up(w_up):
    n_e, d, n = w_up.shape
    src = jnp.arange(n)
    dst = jnp.where(src % 2 == 0, src // 2, n // 2 + src // 2)
    perm = jnp.zeros((n, n), BF16).at[src, dst].set(1.0)
    return pl.pallas_call(
        _deinterleave_kernel,
        grid=(n_e,),
        in_specs=[pl.BlockSpec((1, d, n), lambda e: (e, 0, 0)), pl.BlockSpec((n, n), lambda e: (0, 0))],
        out_specs=pl.BlockSpec((1, d, n), lambda e: (e, 0, 0)),
        out_shape=jax.ShapeDtypeStruct((n_e, d, n), BF16),
        compiler_params=_cparams(("parallel",), 32),
        name="deinterleave_up",
    )(w_up, perm)


def _rope_tables(seq):
    inv = ROPE_THETA ** (-jnp.arange(ROPE_HALF, dtype=F32) / ROPE_HALF)
    ang = jnp.arange(seq, dtype=F32)[:, None] * inv[None, :]
    cos, sin = jnp.cos(ang), jnp.sin(ang)
    pad = SWA_HEAD_DIM - ROPE_DIM
    cos_t = jnp.concatenate([cos, cos, jnp.ones((seq, pad), F32)], axis=1)
    sin_t = jnp.concatenate([-sin, sin, jnp.zeros((seq, pad), F32)], axis=1)
    return cos_t, sin_t


def _prepare(norm_mix, w_in, w_gla_decay, b_gla_decay, gla_out_norm, swa_sink, b_merge, w_proj_gla, w_proj_swa,
             w_out, norm_cross, norm_mem, w_cross_q, w_cross_kv, w_cross_o, norm_moe, w_router, b_router,
             w_up, b_up, w_down, b_down, norm_final):
    p = {}
    lr0 = 2 * GLA_QK_W + 2 * GLA_V_W
    lr1 = lr0 + 2 * GLA_GATE_RANK
    g0 = lr1 + SWA_Q_W + 2 * SWA_KV_W
    p["w_main"] = jnp.concatenate([w_in[:, g0:], w_in[:, :lr0], w_in[:, lr1:g0]], axis=1).astype(BF16)
    p["w_lr"] = jnp.pad(w_in[:, lr0:lr1], ((0, 0), (0, LANES - 2 * GLA_GATE_RANK))).astype(BF16)
    wdec = jnp.zeros((2, LANES, GLA_QK_W), F32)
    for d in range(2):
        wdec = wdec.at[d, d * GLA_GATE_RANK:(d + 1) * GLA_GATE_RANK, :].set(w_gla_decay[d])
    p["w_dec"] = wdec.astype(BF16)
    p["b_dec"] = b_gla_decay.reshape(2, 1, GLA_QK_W)
    p["g_gla"] = gla_out_norm.reshape(1, GLA_V_W)
    p["sink"] = swa_sink
    p["norm_mix"] = norm_mix.reshape(1, D_MODEL)
    p["b_merge"] = b_merge
    p["w_pg"] = w_proj_gla.astype(BF16)
    p["w_ps"] = w_proj_swa.astype(BF16)
    p["w_out"] = w_out.astype(BF16)
    p["norm_cross"] = norm_cross.reshape(1, D_MODEL)
    p["norm_mem"] = norm_mem.reshape(1, D_MODEL)
    p["w_cq"] = w_cross_q.astype(BF16)
    p["w_ckv"] = w_cross_kv.astype(BF16)
    p["w_co"] = w_cross_o.astype(BF16)
    p["norm_moe"] = norm_moe.reshape(1, D_MODEL)
    wr = jnp.pad(w_router, ((0, 0), (0, LANES - N_EXPERTS)))
    wr_hi = wr.astype(BF16)
    p["w_router"] = jnp.stack([wr_hi, (wr - wr_hi.astype(F32)).astype(BF16)])
    p["b_router"] = jnp.pad(b_router, (0, LANES - N_EXPERTS)).reshape(1, LANES)
    p["w_up"] = _deinterleave_up(w_up)
    b_up3 = b_up.reshape(N_EXPERTS, D_FF, 2)
    p["b_up"] = jnp.concatenate([b_up3[:, :, 0], b_up3[:, :, 1]], axis=1).reshape(N_EXPERTS, 1, 2 * D_FF)
    p["w_down"] = w_down.astype(BF16)
    p["b_down"] = jnp.pad(b_down, ((0, LANES - N_EXPERTS), (0, 0))).astype(BF16)
    p["norm_final"] = norm_final.reshape(1, D_MODEL)
    return p


def _mixer_and_cross(x, mem, p):
    bsz, seq, _ = x.shape
    t = bsz * seq
    x2d = x.reshape(t, D_MODEL)
    cos_t, sin_t = _rope_tables(seq)
    proj, lr = _in_projection(x2d, seq, p["norm_mix"], p["w_main"], p["w_lr"], cos_t, sin_t, min(1024, seq))
    proj3 = proj.reshape(bsz, seq, PROJ_COLS)
    heads_per_step = GLA_HEADS if seq <= 2048 else GLA_HEADS // 2
    o_gla = _gla(proj3, lr.reshape(bsz, seq, LANES), p["w_dec"], p["b_dec"], p["g_gla"], heads_per_step)
    o_swa = _swa(proj3, p["sink"], min(512, seq))
    x1 = _merge(x2d, o_gla.reshape(t, GLA_V_W), o_swa.reshape(t, SWA_Q_W), proj, p["b_merge"],
                p["w_pg"], p["w_ps"], p["w_out"], min(512, seq))
    kv = _mem_kv(mem.reshape(bsz * N_MEM, D_MODEL), p["norm_mem"], p["w_ckv"], N_MEM)
    x2 = _cross(x1.reshape(bsz, seq, D_MODEL), kv.reshape(bsz, N_MEM, 2 * CROSS_W), p["norm_cross"],
                p["w_cq"], p["w_co"], min(512, seq))
    return x2.reshape(t, D_MODEL)


def kernel(x_prompt, x_sample, mem_prompt, mem_sample, norm_mix, w_in, w_gla_decay, b_gla_decay, gla_out_norm,
           swa_sink, b_merge, w_proj_gla, w_proj_swa, w_out, norm_cross, norm_mem, w_cross_q, w_cross_kv,
           w_cross_o, norm_moe, w_router, b_router, w_up, b_up, w_down, b_down, norm_final):
    assert norm_mix.shape[0] == 1, "single-layer stack"
    p = _prepare(norm_mix[0], w_in[0], w_gla_decay[0], b_gla_decay[0], gla_out_norm[0], swa_sink[0], b_merge[0],
                 w_proj_gla[0], w_proj_swa[0], w_out[0], norm_cross[0], norm_mem[0], w_cross_q[0],
                 w_cross_kv[0], w_cross_o[0], norm_moe[0], w_router[0], b_router[0], w_up[0], b_up[0],
                 w_down[0], b_down[0], norm_final)
    groups = ((x_prompt, mem_prompt), (x_sample, mem_sample))
    x2 = [_mixer_and_cross(x, mem, p) for x, mem in groups]
    ys = _moe(x2, p)
    return tuple(y.reshape(x.shape) for y, (x, _) in zip(ys, groups))
```

```python
import functools

import jax
import jax.numpy as jnp
from jax import lax
from jax.experimental import pallas as pl
from jax.experimental.pallas import tpu as pltpu

F32 = jnp.float32
BF16 = jnp.bfloat16

D_MODEL = 2048
EPS = 1e-5
N_MEM = 256

GLA_HEADS = 4
GLA_DK = 128
GLA_DV = 256
GLA_QK_W = GLA_HEADS * GLA_DK
GLA_V_W = GLA_HEADS * GLA_DV
GLA_GATE_RANK = 16
GLA_GATE_NORM = 16.0
GLA_CHUNK = 64
GLA_DECAY_BLOCK = 512

SWA_HEADS = 8
SWA_KV_HEADS = 4
SWA_HEAD_DIM = 128
SWA_WINDOW = 128
SWA_Q_W = SWA_HEADS * SWA_HEAD_DIM
SWA_KV_W = SWA_KV_HEADS * SWA_HEAD_DIM
ROPE_THETA = 500000.0
ROPE_DIM = SWA_HEAD_DIM // 4
ROPE_HALF = ROPE_DIM // 2

CROSS_HEADS = 4
CROSS_HEAD_DIM = 128
CROSS_W = CROSS_HEADS * CROSS_HEAD_DIM

N_EXPERTS = 32
TOP_K = 4
D_FF = D_MODEL // 8
SWIGLU_LIMIT = 7.0
SWIGLU_ALPHA = 1.702

LANES = 128

COL_GATES = 0
COL_GQ = 2 * D_MODEL
COL_GK = COL_GQ + GLA_QK_W
COL_GV = COL_GK + GLA_QK_W
COL_GR = COL_GV + GLA_V_W
COL_SQ = COL_GR + GLA_V_W
COL_SK = COL_SQ + SWA_Q_W
COL_SV = COL_SK + SWA_KV_W
PROJ_COLS = COL_SV + SWA_KV_W
PROJ_TN = 1024
J_SQ = COL_SQ // PROJ_TN
J_SKV = COL_SK // PROJ_TN

MIB = 1024 * 1024


def _cparams(semantics, vmem_mib):
    return pltpu.CompilerParams(dimension_semantics=semantics, vmem_limit_bytes=vmem_mib * MIB)


def _rms(xf, g):
    y = xf * lax.rsqrt(jnp.mean(xf * xf, axis=-1, keepdims=True) + EPS)
    return y * g


def _rope(a, partner, cos_ref, sin_ref):
    nh = a.shape[1] // SWA_HEAD_DIM
    cos = jnp.concatenate([cos_ref[...]] * nh, axis=1)
    sin = jnp.concatenate([sin_ref[...]] * nh, axis=1)
    return a * cos + partner * sin


def _inproj_kernel(x_ref, g_ref, w_ref, wlr_ref, cos_ref, sin_ref, swap_ref, proj_ref, lr_ref, h_ref):
    j = pl.program_id(1)

    @pl.when(j == 0)
    def _():
        h = _rms(x_ref[...], g_ref[...]).astype(BF16)
        h_ref[...] = h
        lr_ref[...] = jnp.dot(h, wlr_ref[...], preferred_element_type=F32).astype(BF16)

    acc = jnp.dot(h_ref[...], w_ref[...], preferred_element_type=F32)

    @pl.when(j == J_SQ)
    def _():
        partner = jnp.dot(acc.astype(BF16), swap_ref[...], preferred_element_type=F32)
        proj_ref[...] = _rope(acc, partner, cos_ref, sin_ref).astype(BF16)

    @pl.when(j == J_SKV)
    def _():
        keys = acc[:, :SWA_KV_W]
        partner = jnp.dot(keys.astype(BF16), swap_ref[:SWA_KV_W, :SWA_KV_W], preferred_element_type=F32)
        roped = _rope(keys, partner, cos_ref, sin_ref)
        proj_ref[...] = jnp.concatenate([roped, acc[:, SWA_KV_W:]], axis=1).astype(BF16)

    @pl.when(jnp.logical_and(j != J_SQ, j != J_SKV))
    def _():
        proj_ref[...] = acc.astype(BF16)


def _rope_swap_matrix():
    col = jnp.arange(PROJ_TN, dtype=jnp.int32)
    lane = col % SWA_HEAD_DIM
    src = jnp.where(lane < ROPE_HALF, col + ROPE_HALF, col - ROPE_HALF)
    hit = (col[:, None] == src[None, :]) & (lane[None, :] < ROPE_DIM)
    return hit.astype(BF16)


def _in_projection(x2d, seq, g, w_main, w_lr, cos_t, sin_t, tm):
    t = x2d.shape[0]
    pos_blocks = seq // tm
    swap = _rope_swap_matrix()
    return pl.pallas_call(
        _inproj_kernel,
        grid=(t // tm, PROJ_COLS // PROJ_TN),
        in_specs=[
            pl.BlockSpec((tm, D_MODEL), lambda i, j: (i, 0)),
            pl.BlockSpec((1, D_MODEL), lambda i, j: (0, 0)),
            pl.BlockSpec((D_MODEL, PROJ_TN), lambda i, j: (0, j)),
            pl.BlockSpec((D_MODEL, LANES), lambda i, j: (0, 0)),
            pl.BlockSpec((tm, SWA_HEAD_DIM), lambda i, j: (i % pos_blocks, 0)),
            pl.BlockSpec((tm, SWA_HEAD_DIM), lambda i, j: (i % pos_blocks, 0)),
            pl.BlockSpec((PROJ_TN, PROJ_TN), lambda i, j: (0, 0)),
        ],
        out_specs=[
            pl.BlockSpec((tm, PROJ_TN), lambda i, j: (i, j)),
            pl.BlockSpec((tm, LANES), lambda i, j: (i, 0)),
        ],
        out_shape=[
            jax.ShapeDtypeStruct((t, PROJ_COLS), BF16),
            jax.ShapeDtypeStruct((t, LANES), BF16),
        ],
        scratch_shapes=[pltpu.VMEM((tm, D_MODEL), BF16)],
        compiler_params=_cparams(("parallel", "arbitrary"), 52),
        name="in_projection",
    )(x2d, g, w_main, w_lr, cos_t, sin_t, swap)


def _gla_kernel(q_ref, k_ref, v_ref, r_ref, lr_ref, wdec_ref, bdec_ref, gout_ref, o_ref,
                oacc_ref, state_ref, b_ref):
    seq = q_ref.shape[1]
    heads = q_ref.shape[2] // GLA_DK
    c = GLA_CHUNK
    n_chunks = seq // c
    half = n_chunks // 2
    scale = GLA_DK ** -0.5

    state_ref[...] = jnp.zeros_like(state_ref)

    row = lax.broadcasted_iota(jnp.int32, (c, c), 0)
    col = lax.broadcasted_iota(jnp.int32, (c, c), 1)
    cum_mat = (jnp.where(col <= row, 1.0, 0.0).astype(BF16), jnp.where(col >= row, 1.0, 0.0).astype(BF16))
    att_mask = (col <= row, col > row)

    kw = heads * GLA_DK

    def decay_block(j, carry):
        r0 = pl.multiple_of(j * GLA_DECAY_BLOCK, GLA_DECAY_BLOCK)
        lr = lr_ref[0, pl.ds(r0, GLA_DECAY_BLOCK), :]
        zs = [jnp.dot(lr, wdec_ref[d], preferred_element_type=F32) + bdec_ref[d] for d in range(2)]
        for d in range(2):
            log_a = (jnp.minimum(zs[d], 0.0) - jnp.log(1.0 + jnp.exp(-jnp.abs(zs[d])))) / GLA_GATE_NORM
            la_hi = log_a.astype(BF16)
            la_lo = (log_a - la_hi.astype(F32)).astype(BF16)
            for m in range(GLA_DECAY_BLOCK // c):
                sl = slice(m * c, (m + 1) * c)
                b = (jnp.dot(cum_mat[d], la_hi[sl], preferred_element_type=F32)
                     + jnp.dot(cum_mat[d], la_lo[sl], preferred_element_type=F32))
                b_ref[pl.ds(r0 + m * c, c), d * kw:(d + 1) * kw] = b
        return carry

    lax.fori_loop(0, seq // GLA_DECAY_BLOCK, decay_block, 0)

    def advance(i):
        rows = [pl.ds(pl.multiple_of(n * c, c), c) for n in (i, n_chunks - 1 - i)]
        work = []
        for d in range(2):
            b = b_ref[rows[d], d * kw:(d + 1) * kw]
            b_edge = b[c - 1:c, :] if d == 0 else b[0:1, :]
            q = q_ref[0, rows[d], :].astype(F32) * scale
            k = k_ref[0, rows[d], :].astype(F32)
            q_dec = (q * jnp.exp(b)).astype(BF16)
            k_dec = (k * jnp.exp(-b)).astype(BF16)
            k_end = (k * jnp.exp(b_edge - b)).astype(BF16)
            decay = jnp.exp(b_edge)
            for hh in range(heads):
                kc = slice(hh * GLA_DK, (hh + 1) * GLA_DK)
                att = lax.dot_general(q_dec[:, kc], k_dec[:, kc], (((1,), (1,)), ((), ())),
                                      preferred_element_type=F32)
                att = jnp.where(att_mask[d], att, 0.0).astype(BF16)
                work.append((d, hh, q_dec[:, kc], k_end[:, kc], decay[:, kc], att))
        outs = []
        for d, hh, q_dec, k_end, decay, att in work:
            vcols = slice(hh * GLA_DV, (hh + 1) * GLA_DV)
            v = v_ref[0, rows[d], vcols]
            slot = d * heads + hh
            state = state_ref[slot]
            o = jnp.dot(att, v, preferred_element_type=F32)
            o = o + lax.dot_general(q_dec, state.astype(BF16), (((1,), (1,)), ((), ())),
                                    preferred_element_type=F32)
            kv = lax.dot_general(v, k_end, (((0,), (0,)), ((), ())), preferred_element_type=F32)
            state_ref[slot] = decay * state + kv
            outs.append((rows[d], vcols, o))
        return outs

    def finalize(rows, vcols, o):
        o = o * lax.rsqrt(jnp.mean(o * o, axis=-1, keepdims=True) + EPS) * gout_ref[:, vcols]
        r = r_ref[0, rows, vcols].astype(F32)
        o_ref[0, rows, vcols] = (o * (r * jax.nn.sigmoid(r))).astype(BF16)

    def first_half(i, carry):
        for rows, vcols, o in advance(i):
            oacc_ref[rows, vcols] = o
        return carry

    def second_half(i, carry):
        for rows, vcols, o in advance(i):
            finalize(rows, vcols, oacc_ref[rows, vcols] + o)
        return carry

    lax.fori_loop(0, half, first_half, 0, unroll=2)
    lax.fori_loop(half, n_chunks, second_half, 0, unroll=2)


def _gla(proj3, lr3, wdec_pad, bdec, gout, heads_per_step):
    bsz, seq, _ = proj3.shape
    hp = heads_per_step
    kw, vw = hp * GLA_DK, hp * GLA_DV
    qb, kb = COL_GQ // kw, COL_GK // kw
    vb, rb = COL_GV // vw, COL_GR // vw
    return pl.pallas_call(
        _gla_kernel,
        grid=(bsz, GLA_HEADS // hp),
        in_specs=[
            pl.BlockSpec((1, seq, kw), lambda b, h: (b, 0, qb + h)),
            pl.BlockSpec((1, seq, kw), lambda b, h: (b, 0, kb + h)),
            pl.BlockSpec((1, seq, vw), lambda b, h: (b, 0, vb + h)),
            pl.BlockSpec((1, seq, vw), lambda b, h: (b, 0, rb + h)),
            pl.BlockSpec((1, seq, LANES), lambda b, h: (b, 0, 0)),
            pl.BlockSpec((2, LANES, kw), lambda b, h: (0, 0, h)),
            pl.BlockSpec((2, 1, kw), lambda b, h: (0, 0, h)),
            pl.BlockSpec((1, vw), lambda b, h: (0, h)),
        ],
        out_specs=pl.BlockSpec((1, seq, vw), lambda b, h: (b, 0, h)),
        out_shape=jax.ShapeDtypeStruct((bsz, seq, GLA_V_W), BF16),
        scratch_shapes=[
            pltpu.VMEM((seq, vw), F32),
            pltpu.VMEM((2 * hp, GLA_DV, GLA_DK), F32),
            pltpu.VMEM((seq, 2 * kw), F32),
        ],
        compiler_params=_cparams(("parallel", "parallel"), 56),
        name="gla",
    )(proj3, proj3, proj3, proj3, lr3, wdec_pad, bdec, gout)


def _swa_kernel(sink_ref, q_ref, kp_ref, kc_ref, kn_ref, vp_ref, vc_ref, vn_ref, o_ref, kbuf_ref, vbuf_ref):
    t = pl.program_id(1)
    tq = q_ref.shape[1]
    w = SWA_WINDOW
    hd = SWA_HEAD_DIM
    groups = SWA_HEADS // SWA_KV_HEADS
    seq = pl.num_programs(1) * tq
    scale = hd ** -0.5

    kbuf_ref[0:w, :] = kp_ref[0]
    kbuf_ref[w:w + tq, :] = kc_ref[0]
    kbuf_ref[w + tq:, :] = kn_ref[0]
    vbuf_ref[0:w, :] = vp_ref[0]
    vbuf_ref[w:w + tq, :] = vc_ref[0]
    vbuf_ref[w + tq:, :] = vn_ref[0]

    qi = lax.broadcasted_iota(jnp.int32, (groups * w, 3 * w), 0) & (w - 1)
    kj = lax.broadcasted_iota(jnp.int32, (groups * w, 3 * w), 1)
    in_window = jnp.abs(kj - w - qi) <= w

    def block(m, carry):
        q0 = pl.multiple_of(m * w, w)
        kpos = t * tq + q0 - w + kj
        valid = in_window & (kpos >= 0) & (kpos < seq)
        row_g = lax.broadcasted_iota(jnp.int32, (groups * w, 1), 0) // w
        scores = []
        for kh in range(SWA_KV_HEADS):
            qs = jnp.concatenate(
                [q_ref[0, pl.ds(q0, w), (kh * groups + g) * hd:(kh * groups + g + 1) * hd] for g in range(groups)],
                axis=0)
            ks = kbuf_ref[pl.ds(q0, 3 * w), kh * hd:(kh + 1) * hd]
            scores.append(lax.dot_general(qs, ks, (((1,), (1,)), ((), ())), preferred_element_type=F32))
        probs = []
        for kh in range(SWA_KV_HEADS):
            s = jnp.where(valid, scores[kh] * scale, -jnp.inf)
            sink = jnp.zeros((groups * w, 1), F32)
            for g in range(groups):
                sink = jnp.where(row_g == g, sink_ref[kh * groups + g], sink)
            mx = jnp.maximum(jnp.max(s, axis=-1, keepdims=True), sink)
            p = jnp.exp(s - mx)
            denom = jnp.sum(p, axis=-1, keepdims=True) + jnp.exp(sink - mx)
            probs.append((p.astype(BF16), denom))
        for kh in range(SWA_KV_HEADS):
            p, denom = probs[kh]
            vs = vbuf_ref[pl.ds(q0, 3 * w), kh * hd:(kh + 1) * hd]
            o = jnp.dot(p, vs, preferred_element_type=F32) / denom
            for g in range(groups):
                head = kh * groups + g
                o_ref[0, pl.ds(q0, w), head * hd:(head + 1) * hd] = o[g * w:(g + 1) * w, :].astype(BF16)
        return carry

    lax.fori_loop(0, tq // w, block, 0)


def _swa(proj3, sink, tq):
    bsz, seq, _ = proj3.shape
    w = SWA_WINDOW
    r = tq // w
    last = seq // w - 1
    qb = COL_SQ // SWA_Q_W
    kb, vb = COL_SK // SWA_KV_W, COL_SV // SWA_KV_W

    def neighbours(cb):
        return [
            pl.BlockSpec((1, w, SWA_KV_W), lambda b, t: (b, jnp.maximum(t * r - 1, 0), cb)),
            pl.BlockSpec((1, tq, SWA_KV_W), lambda b, t: (b, t, cb)),
            pl.BlockSpec((1, w, SWA_KV_W), lambda b, t: (b, jnp.minimum((t + 1) * r, last), cb)),
        ]

    return pl.pallas_call(
        _swa_kernel,
        grid=(bsz, seq // tq),
        in_specs=[pl.BlockSpec(memory_space=pltpu.SMEM),
                  pl.BlockSpec((1, tq, SWA_Q_W), lambda b, t: (b, t, qb))] + neighbours(kb) + neighbours(vb),
        out_specs=pl.BlockSpec((1, tq, SWA_Q_W), lambda b, t: (b, t, 0)),
        out_shape=jax.ShapeDtypeStruct((bsz, seq, SWA_Q_W), BF16),
        scratch_shapes=[pltpu.VMEM((tq + 2 * w, SWA_KV_W), BF16), pltpu.VMEM((tq + 2 * w, SWA_KV_W), BF16)],
        compiler_params=_cparams(("parallel", "parallel"), 32),
        name="swa",
    )(sink, proj3, proj3, proj3, proj3, proj3, proj3, proj3)


def _merge_kernel(x_ref, og_ref, os_ref, g0_ref, g1_ref, bm_ref, wg_ref, ws_ref, wo_ref, o_ref):
    a = jnp.dot(og_ref[...], wg_ref[...], preferred_element_type=F32)
    b = jnp.dot(os_ref[...], ws_ref[...], preferred_element_type=F32)
    g0 = jax.nn.sigmoid(g0_ref[...].astype(F32) + bm_ref[0:1, :])
    g1 = jax.nn.sigmoid(g1_ref[...].astype(F32) + bm_ref[1:2, :])
    merged = (g0 * a + g1 * b).astype(BF16)
    o_ref[...] = x_ref[...] + jnp.dot(merged, wo_ref[...], preferred_element_type=F32)


def _resident(shape):
    return pl.BlockSpec(shape, lambda *_: (0,) * len(shape), pipeline_mode=pl.Buffered(1))


def _merge(x2d, o_gla, o_swa, proj, b_merge, w_pg, w_ps, w_out, tm):
    t = x2d.shape[0]
    return pl.pallas_call(
        _merge_kernel,
        grid=(t // tm,),
        in_specs=[
            pl.BlockSpec((tm, D_MODEL), lambda i: (i, 0)),
            pl.BlockSpec((tm, GLA_V_W), lambda i: (i, 0)),
            pl.BlockSpec((tm, SWA_Q_W), lambda i: (i, 0)),
            pl.BlockSpec((tm, D_MODEL), lambda i: (i, 0)),
            pl.BlockSpec((tm, D_MODEL), lambda i: (i, 1)),
            _resident((2, D_MODEL)),
            _resident((GLA_V_W, D_MODEL)),
            _resident((SWA_Q_W, D_MODEL)),
            _resident((D_MODEL, D_MODEL)),
        ],
        out_specs=pl.BlockSpec((tm, D_MODEL), lambda i: (i, 0)),
        out_shape=jax.ShapeDtypeStruct((t, D_MODEL), F32),
        compiler_params=_cparams(("parallel",), 56),
        name="merge",
    )(x2d, o_gla, o_swa, proj, proj, b_merge, w_pg, w_ps, w_out)


def _norm_matmul_kernel(x_ref, g_ref, w_ref, o_ref):
    h = _rms(x_ref[...], g_ref[...]).astype(BF16)
    o_ref[...] = jnp.dot(h, w_ref[...], preferred_element_type=F32).astype(o_ref.dtype)


def _mem_kv(mem2d, g, w_kv, tm):
    t = mem2d.shape[0]
    n = w_kv.shape[1]
    return pl.pallas_call(
        _norm_matmul_kernel,
        grid=(t // tm,),
        in_specs=[
            pl.BlockSpec((tm, D_MODEL), lambda i: (i, 0)),
            _resident((1, D_MODEL)),
            _resident((D_MODEL, n)),
        ],
        out_specs=pl.BlockSpec((tm, n), lambda i: (i, 0)),
        out_shape=jax.ShapeDtypeStruct((t, n), BF16),
        compiler_params=_cparams(("parallel",), 32),
        name="mem_kv",
    )(mem2d, g, w_kv)


def _cross_kernel(x_ref, g_ref, kv_ref, wq_ref, wo_ref, o_ref):
    x = x_ref[0]
    h = _rms(x, g_ref[...]).astype(BF16)
    q = jnp.dot(h, wq_ref[...], preferred_element_type=F32).astype(BF16)
    hd = CROSS_HEAD_DIM
    scale = hd ** -0.5
    scores = [lax.dot_general(q[:, head * hd:(head + 1) * hd], kv_ref[0, :, head * hd:(head + 1) * hd],
                              (((1,), (1,)), ((), ())), preferred_element_type=F32)
              for head in range(CROSS_HEADS)]
    probs = []
    for s in scores:
        s = s * scale
        p = jnp.exp(s - jnp.max(s, axis=-1, keepdims=True))
        probs.append((p.astype(BF16), jnp.sum(p, axis=-1, keepdims=True)))
    outs = []
    for head, (p, denom) in enumerate(probs):
        vh = kv_ref[0, :, CROSS_W + head * hd:CROSS_W + (head + 1) * hd]
        outs.append((jnp.dot(p, vh, preferred_element_type=F32) / denom).astype(BF16))
    o = jnp.concatenate(outs, axis=1)
    o_ref[0] = x + jnp.dot(o, wo_ref[...], preferred_element_type=F32)


def _cross(x3, kv3, g, w_q, w_o, tm):
    bsz, seq, _ = x3.shape
    return pl.pallas_call(
        _cross_kernel,
        grid=(bsz, seq // tm),
        in_specs=[
            pl.BlockSpec((1, tm, D_MODEL), lambda b, i: (b, i, 0)),
            _resident((1, D_MODEL)),
            pl.BlockSpec((1, N_MEM, 2 * CROSS_W), lambda b, i: (b, 0, 0)),
            _resident((D_MODEL, CROSS_W)),
            _resident((CROSS_W, D_MODEL)),
        ],
        out_specs=pl.BlockSpec((1, tm, D_MODEL), lambda b, i: (b, i, 0)),
        out_shape=jax.ShapeDtypeStruct((bsz, seq, D_MODEL), F32),
        compiler_params=_cparams(("parallel", "parallel"), 48),
        name="cross_attention",
    )(x3, g, kv3, w_q, w_o)


HALF_D = D_MODEL // 2
COMBINE_TM = 256
ROUTE_TM = COMBINE_TM
DISPATCH_TM = 512
EXPERT_ROWS = 512


def _pack_rows(a):
    return pltpu.pack_elementwise([a[:, :HALF_D], a[:, HALF_D:]], packed_dtype=BF16)


def _unpack_rows(w):
    lo = pltpu.unpack_elementwise(w, index=0, packed_dtype=BF16, unpacked_dtype=F32)
    hi = pltpu.unpack_elementwise(w, index=1, packed_dtype=BF16, unpacked_dtype=F32)
    return jnp.concatenate([lo, hi], axis=1)


def _route_kernel(x_ref, g_ref, wr_ref, br_ref, cnt0_ref, hp_ref, er_ref, within_ref, comb_ref, cnt_ref,
                  first_ref, tcnt_ref, run_ref):
    i = pl.program_id(0)
    tm = x_ref.shape[0]
    lane = lax.broadcasted_iota(jnp.int32, (tm, LANES), 1)

    @pl.when(i == 0)
    def _():
        run_ref[...] = cnt0_ref[...]

    hf = _rms(x_ref[...], g_ref[...])
    h_hi = hf.astype(BF16)
    h_lo = (hf - h_hi.astype(F32)).astype(BF16)
    logits = (jnp.dot(h_hi, wr_ref[0], preferred_element_type=F32)
              + jnp.dot(h_lo, wr_ref[0], preferred_element_type=F32)
              + jnp.dot(h_hi, wr_ref[1], preferred_element_type=F32)) + br_ref[...]
    logits = jnp.where(lane < N_EXPERTS, logits, -jnp.inf)
    vals, hots, idxs = [], [], []
    for _ in range(TOP_K):
        mx = jnp.max(logits, axis=-1, keepdims=True)
        idx = jnp.min(jnp.where(logits == mx, lane, LANES), axis=-1, keepdims=True)
        hot = lane == idx
        vals.append(mx)
        hots.append(hot)
        idxs.append(idx)
        logits = jnp.where(hot, -jnp.inf, logits)
    exps = [jnp.exp(v - vals[0]) for v in vals]
    denom = exps[0] + exps[1] + exps[2] + exps[3]
    sel = jnp.zeros((tm, LANES), F32)
    comb = jnp.zeros((tm, LANES), F32)
    for k in range(TOP_K):
        sel = sel + jnp.where(hots[k], 1.0, 0.0)
        comb = comb + jnp.where(hots[k], exps[k] / denom, 0.0)
    comb_ref[...] = comb

    row = lax.broadcasted_iota(jnp.int32, (tm, tm), 0)
    col = lax.broadcasted_iota(jnp.int32, (tm, tm), 1)
    earlier = jnp.where(col < row, 1.0, 0.0).astype(BF16)
    within = jnp.dot(earlier, sel.astype(BF16), preferred_element_type=F32)
    rank = within + run_ref[...]
    tile_count = jnp.sum(sel, axis=0, keepdims=True)
    within_ref[...] = within
    first_ref[0] = run_ref[...]
    tcnt_ref[0] = tile_count
    run_ref[...] = run_ref[...] + tile_count
    cnt_ref[...] = run_ref[...]

    table = jnp.zeros((tm, LANES), jnp.int32)
    for k in range(TOP_K):
        rank_k = jnp.sum(jnp.where(hots[k], rank, 0.0), axis=-1, keepdims=True).astype(jnp.int32)
        table = jnp.where(lane == k, idxs[k], table)
        table = jnp.where(lane == TOP_K + k, rank_k, table)
    er_ref[...] = jnp.transpose(table)[:2 * TOP_K, :]
    hp_ref[...] = _pack_rows(hf)


def _route(x2d, g, w_router, b_router, cnt0):
    t = x2d.shape[0]
    tm = ROUTE_TM
    const = lambda *shape: pl.BlockSpec(shape, lambda i: (0,) * len(shape))
    return pl.pallas_call(
        _route_kernel,
        grid=(t // tm,),
        in_specs=[
            pl.BlockSpec((tm, D_MODEL), lambda i: (i, 0)),
            const(1, D_MODEL),
            const(2, D_MODEL, LANES),
            const(1, LANES),
            const(1, LANES),
        ],
        out_specs=[
            pl.BlockSpec((tm, HALF_D), lambda i: (i, 0)),
            pl.BlockSpec((2 * TOP_K, tm), lambda i: (0, i)),
            pl.BlockSpec((tm, LANES), lambda i: (i, 0)),
            pl.BlockSpec((tm, LANES), lambda i: (i, 0)),
            const(1, LANES),
            pl.BlockSpec((1, 1, LANES), lambda i: (i, 0, 0)),
            pl.BlockSpec((1, 1, LANES), lambda i: (i, 0, 0)),
        ],
        out_shape=[
            jax.ShapeDtypeStruct((t, HALF_D), jnp.uint32),
            jax.ShapeDtypeStruct((2 * TOP_K, t), jnp.int32),
            jax.ShapeDtypeStruct((t, LANES), F32),
            jax.ShapeDtypeStruct((t, LANES), F32),
            jax.ShapeDtypeStruct((1, LANES), F32),
            jax.ShapeDtypeStruct((t // tm, 1, LANES), F32),
            jax.ShapeDtypeStruct((t // tm, 1, LANES), F32),
        ],
        scratch_shapes=[pltpu.VMEM((1, LANES), F32)],
        compiler_params=_cparams(("arbitrary",), 40),
        name="moe_route",
    )(x2d, g, w_router, b_router, cnt0)


ROW_DMA_UNROLL = 8


def _dispatch_kernel(tv_ref, pos_ref, hp_ref, hs_ref, zero_ref, sem, zero_sem):
    tm = hp_ref.shape[0]

    @pl.when(pl.program_id(0) == 0)
    def _():
        zero_ref[...] = _pack_rows(jnp.zeros((zero_ref.shape[0], D_MODEL), F32))

        def tile_fill(g):
            r0 = pl.multiple_of(g * EXPERT_ROWS, EXPERT_ROWS)
            return pltpu.make_async_copy(zero_ref, hs_ref.at[pl.ds(r0, EXPERT_ROWS), :], zero_sem)

        def start_fill(g, carry):
            @pl.when(tv_ref[g] < EXPERT_ROWS)
            def _():
                tile_fill(g).start()
            return carry

        def wait_fill(g, carry):
            @pl.when(tv_ref[g] < EXPERT_ROWS)
            def _():
                tile_fill(g).wait()
            return carry

        n_tiles = hs_ref.shape[0] // EXPERT_ROWS
        lax.fori_loop(0, n_tiles, start_fill, 0)
        lax.fori_loop(0, n_tiles, wait_fill, 0)

    def row_copy(t, k):
        return pltpu.make_async_copy(hp_ref.at[pl.ds(t, 1), :], hs_ref.at[pl.ds(pos_ref[k, t], 1), :], sem)

    def issue(t, carry):
        for k in range(TOP_K):
            row_copy(t, k).start()
        return carry

    def drain(t, carry):
        for k in range(TOP_K):
            row_copy(t, k).wait()
        return carry

    lax.fori_loop(0, tm, issue, 0, unroll=ROW_DMA_UNROLL)
    lax.fori_loop(0, tm, drain, 0, unroll=ROW_DMA_UNROLL)


def _dispatch(tile_valid, pos, hp, n_rows):
    t = hp.shape[0]
    tm = DISPATCH_TM
    return pl.pallas_call(
        _dispatch_kernel,
        grid=(t // tm,),
        in_specs=[
            pl.BlockSpec(memory_space=pltpu.SMEM),
            pl.BlockSpec((TOP_K, tm), lambda i: (0, i), memory_space=pltpu.SMEM),
            pl.BlockSpec((tm, HALF_D), lambda i: (i, 0)),
        ],
        out_specs=pl.BlockSpec(memory_space=pl.ANY),
        out_shape=jax.ShapeDtypeStruct((n_rows, HALF_D), jnp.uint32),
        scratch_shapes=[pltpu.VMEM((EXPERT_ROWS, HALF_D), jnp.uint32), pltpu.SemaphoreType.DMA,
                        pltpu.SemaphoreType.DMA],
        compiler_params=_cparams(("arbitrary",), 32),
        name="moe_dispatch",
    )(tile_valid, pos, hp)


def _expert_kernel(te_ref, tv_ref, tb_ref, hs_ref, wup_ref, bup_ref, wdn_ref, ys_ref):
    g = pl.program_id(0)
    valid = tv_ref[g]

    @pl.when(valid == 0)
    def _():
        ys_ref[...] = jnp.zeros_like(ys_ref)

    @pl.when(valid > 0)
    def _():
        x = _unpack_rows(hs_ref[...]).astype(BF16)
        up = jnp.dot(x, wup_ref[0], preferred_element_type=F32) + bup_ref[0]
        glu = jnp.minimum(up[:, :D_FF], SWIGLU_LIMIT)
        lin = jnp.clip(up[:, D_FF:], -SWIGLU_LIMIT, SWIGLU_LIMIT)
        act = glu * jax.nn.sigmoid(SWIGLU_ALPHA * glu) * (lin + 1.0)
        y = jnp.dot(act.astype(BF16), wdn_ref[0], preferred_element_type=F32)
        ys_ref[...] = y.astype(BF16)


def _experts(tile_expert, tile_valid, tile_block, hs, w_up, b_up, w_down):
    n_rows = hs.shape[0]
    rows = EXPERT_ROWS
    grid_spec = pltpu.PrefetchScalarGridSpec(
        num_scalar_prefetch=3,
        grid=(n_rows // rows,),
        in_specs=[
            pl.BlockSpec((rows, HALF_D), lambda g, te, tv, tb: (tb[g], 0)),
            pl.BlockSpec((1, D_MODEL, 2 * D_FF), lambda g, te, tv, tb: (te[g], 0, 0)),
            pl.BlockSpec((1, 1, 2 * D_FF), lambda g, te, tv, tb: (te[g], 0, 0)),
            pl.BlockSpec((1, D_FF, D_MODEL), lambda g, te, tv, tb: (te[g], 0, 0)),
        ],
        out_specs=pl.BlockSpec((rows, D_MODEL), lambda g, te, tv, tb: (g, 0)),
    )
    return pl.pallas_call(
        _expert_kernel,
        grid_spec=grid_spec,
        out_shape=jax.ShapeDtypeStruct((n_rows, D_MODEL), BF16),
        compiler_params=_cparams(("arbitrary",), 32),
        name="moe_experts",
    )(tile_expert, tile_valid, tile_block, hs, w_up, b_up, w_down)


COMBINE_WIN = 64
ROW_ALIGN = 16


def _combine_kernel(ws_ref, nx_ref, x_ref, within_ref, comb_ref, shift_ref, expand_ref, bdn_ref, gf_ref, ys_ref,
                    o_ref, win0_ref, win1_ref, extra_ref, acc_ref, sem0, sem1, extra_sem, *, tile_offset):
    i = pl.program_id(0)
    last = pl.num_programs(0) - 1
    tm = x_ref.shape[0]
    w = COMBINE_WIN
    tile = tile_offset + i
    next_tile = tile_offset + jnp.minimum(i + 1, last)

    def step(own_buf, own_sem, spare_buf, spare_sem):
        def window(t, e, buf, sem):
            r0 = pl.multiple_of(ws_ref[t * N_EXPERTS + e], ROW_ALIGN)
            return pltpu.make_async_copy(ys_ref.at[pl.ds(r0, w), :], buf.at[pl.ds(e * w, w), :], sem)

        @pl.when(i == 0)
        def _():
            for e in range(N_EXPERTS):
                window(tile, e, own_buf, own_sem).start()

        for e in range(N_EXPERTS):
            window(next_tile, e, spare_buf, spare_sem).start()

        rel = within_ref[...] + shift_ref[0]
        comb = comb_ref[...]
        rel_b = jnp.dot(rel.astype(BF16), expand_ref[...], preferred_element_type=F32)
        gate_b = jnp.dot(comb.astype(BF16), expand_ref[...], preferred_element_type=F32)
        slot = (lax.broadcasted_iota(jnp.int32, rel_b.shape, 1) & (w - 1)).astype(F32)
        place = jnp.where(rel_b == slot, gate_b, 0.0).astype(BF16)
        acc = x_ref[...] + jnp.dot(comb.astype(BF16), bdn_ref[...], preferred_element_type=F32)

        for e in range(N_EXPERTS):
            window(tile, e, own_buf, own_sem).wait()
        acc_ref[...] = acc + jnp.dot(place, own_buf[...], preferred_element_type=F32)

        def overflow(e, carry):
            more = nx_ref[tile * N_EXPERTS + e]

            @pl.when(more > 0)
            def _():
                lane = lax.broadcasted_iota(jnp.int32, (tm, LANES), 1)
                rel_e = jnp.sum(jnp.where(lane == e, within_ref[...] + shift_ref[0], 0.0), axis=-1, keepdims=True)
                gate_e = jnp.sum(jnp.where(lane == e, comb_ref[...], 0.0), axis=-1, keepdims=True)
                r0 = pl.multiple_of(ws_ref[tile * N_EXPERTS + e], ROW_ALIGN)
                cols = lax.broadcasted_iota(jnp.int32, (tm, w), 1).astype(F32)

                def further(n, inner):
                    cp = pltpu.make_async_copy(ys_ref.at[pl.ds(r0 + n * w, w), :], extra_ref, extra_sem)
                    cp.start()
                    cp.wait()
                    hit = jnp.where(rel_e - (n * w).astype(F32) == cols, gate_e, 0.0).astype(BF16)
                    acc_ref[...] += jnp.dot(hit, extra_ref[...], preferred_element_type=F32)
                    return inner

                lax.fori_loop(1, more + 1, further, 0)

            return carry

        lax.fori_loop(0, N_EXPERTS, overflow, 0)

        o_ref[...] = _rms(acc_ref[...], gf_ref[...])

        @pl.when(i == last)
        def _():
            for e in range(N_EXPERTS):
                window(next_tile, e, spare_buf, spare_sem).wait()

    @pl.when(lax.rem(i, 2) == 0)
    def _():
        step(win0_ref, sem0, win1_ref, sem1)

    @pl.when(lax.rem(i, 2) == 1)
    def _():
        step(win1_ref, sem1, win0_ref, sem0)


def _combine(win_start, win_more, win_shift, tile_offset, x2d, within, comb, expand, b_down, g_final, ys):
    t = x2d.shape[0]
    tm = COMBINE_TM
    const = lambda *shape: pl.BlockSpec(shape, lambda i: (0,) * len(shape))
    smem = pl.BlockSpec(memory_space=pltpu.SMEM)
    return pl.pallas_call(
        functools.partial(_combine_kernel, tile_offset=tile_offset),
        grid=(t // tm,),
        in_specs=[
            smem,
            smem,
            pl.BlockSpec((tm, D_MODEL), lambda i: (i, 0)),
            pl.BlockSpec((tm, LANES), lambda i: (i, 0)),
            pl.BlockSpec((tm, LANES), lambda i: (i, 0)),
            pl.BlockSpec((1, 1, LANES), lambda i: (tile_offset + i, 0, 0)),
            const(LANES, N_EXPERTS * COMBINE_WIN),
            const(LANES, D_MODEL),
            const(1, D_MODEL),
            pl.BlockSpec(memory_space=pl.ANY),
        ],
        out_specs=pl.BlockSpec((tm, D_MODEL), lambda i: (i, 0)),
        out_shape=jax.ShapeDtypeStruct((t, D_MODEL), F32),
        scratch_shapes=[pltpu.VMEM((N_EXPERTS * COMBINE_WIN, D_MODEL), BF16),
                        pltpu.VMEM((N_EXPERTS * COMBINE_WIN, D_MODEL), BF16),
                        pltpu.VMEM((COMBINE_WIN, D_MODEL), BF16),
                        pltpu.VMEM((tm, D_MODEL), F32),
                        pltpu.SemaphoreType.DMA, pltpu.SemaphoreType.DMA, pltpu.SemaphoreType.DMA],
        compiler_params=_cparams(("arbitrary",), 48),
        name="moe_combine",
    )(win_start, win_more, x2d, within, comb, win_shift, expand, b_down, g_final, ys)


def _tile_plan(counts, n_tiles):
    rows = EXPERT_ROWS
    tiles_e = (counts + rows - 1) // rows
    tile_end = jnp.cumsum(tiles_e)
    tile_start = tile_end - tiles_e
    total = tile_end[-1]
    g = jnp.arange(n_tiles, dtype=jnp.int32)
    last = total - 1
    g_eff = jnp.minimum(g, last)
    te = jnp.sum((tile_end[None, :] <= g_eff[:, None]).astype(jnp.int32), axis=1)
    te = jnp.minimum(te, N_EXPERTS - 1)
    valid = jnp.clip(counts[te] - (g_eff - tile_start[te]) * rows, 0, rows)
    valid = jnp.where(g < total, valid, 0).astype(jnp.int32)
    return (tile_start * rows).astype(jnp.int32), te, valid, g_eff.astype(jnp.int32)


def _moe(x2_groups, p):
    n_tok = sum(x.shape[0] for x in x2_groups)
    n_tiles = n_tok * TOP_K // EXPERT_ROWS + N_EXPERTS + 1
    cnt = jnp.zeros((1, LANES), F32)
    routed = []
    for x2d in x2_groups:
        hp, er, within, comb, cnt, first, tile_cnt = _route(x2d, p["norm_moe"], p["w_router"], p["b_router"], cnt)
        routed.append((hp, er, within, comb, first, tile_cnt))
    counts = cnt[0, :N_EXPERTS].astype(jnp.int32)
    group_start, tile_expert, tile_valid, tile_block = _tile_plan(counts, n_tiles)
    hp_all = jnp.concatenate([r[0] for r in routed], axis=0)
    er_all = jnp.concatenate([r[1] for r in routed], axis=1)
    first = jnp.concatenate([r[4] for r in routed], axis=0)[:, 0, :N_EXPERTS].astype(jnp.int32)
    tile_cnt = jnp.concatenate([r[5] for r in routed], axis=0)[:, 0, :N_EXPERTS].astype(jnp.int32)
    seg_start = group_start[None, :] + first
    win_start = (seg_start // ROW_ALIGN) * ROW_ALIGN
    shift = seg_start - win_start
    win_more = jnp.maximum((shift + tile_cnt + COMBINE_WIN - 1) // COMBINE_WIN - 1, 0)
    win_shift = jnp.pad(shift.astype(F32), ((0, 0), (0, LANES - N_EXPERTS)))[:, None, :]
    lane_expert = jnp.arange(N_EXPERTS * COMBINE_WIN, dtype=jnp.int32) // COMBINE_WIN
    expand = (jnp.arange(LANES, dtype=jnp.int32)[:, None] == lane_expert[None, :]).astype(BF16)
    experts = jnp.arange(N_EXPERTS, dtype=jnp.int32)[:, None, None]
    pos = jnp.sum(jnp.where(er_all[None, :TOP_K] == experts, group_start[:, None, None], 0), axis=0) + er_all[TOP_K:]
    hs = _dispatch(tile_valid, pos, hp_all, n_tiles * EXPERT_ROWS)
    ys = _experts(tile_expert, tile_valid, tile_block, hs, p["w_up"], p["b_up"], p["w_down"])
    outs, tile_offset = [], 0
    for x2d, (_, _, within, comb, _, _) in zip(x2_groups, routed):
        outs.append(_combine(win_start.reshape(-1), win_more.reshape(-1), win_shift, tile_offset, x2d, within, comb,
                             expand, p["b_down"], p["norm_final"], ys))
        tile_offset += x2d.shape[0] // COMBINE_TM
    return outs


def _deinterleave_kernel(w_ref, perm_ref, o_ref):
    o_ref[0] = jnp.dot(w_ref[0].astype(BF16), perm_ref[...], preferred_element_type=F32).astype(BF16)


def _deinterleave_up(w_up):
    n_e, d, n = w_up.shape
    src = jnp.arange(n)
    dst = jnp.where(src % 2 == 0, src // 2, n // 2 + src // 2)
    perm = jnp.zeros((n, n), BF16).at[src, dst].set(1.0)
    return pl.pallas_call(
        _deinterleave_kernel,
        grid=(n_e,),
        in_specs=[pl.BlockSpec((1, d, n), lambda e: (e, 0, 0)), pl.BlockSpec((n, n), lambda e: (0, 0))],
        out_specs=pl.BlockSpec((1, d, n), lambda e: (e, 0, 0)),
        out_shape=jax.ShapeDtypeStruct((n_e, d, n), BF16),
        compiler_params=_cparams(("parallel",), 32),
        name="deinterleave_up",
    )(w_up, perm)


def _rope_tables(seq):
    inv = ROPE_THETA ** (-jnp.arange(ROPE_HALF, dtype=F32) / ROPE_HALF)
    ang = jnp.arange(seq, dtype=F32)[:, None] * inv[None, :]
    cos, sin = jnp.cos(ang), jnp.sin(ang)
    pad = SWA_HEAD_DIM - ROPE_DIM
    cos_t = jnp.concatenate([cos, cos, jnp.ones((seq, pad), F32)], axis=1)
    sin_t = jnp.concatenate([-sin, sin, jnp.zeros((seq, pad), F32)], axis=1)
    return cos_t, sin_t


def _prepare(norm_mix, w_in, w_gla_decay, b_gla_decay, gla_out_norm, swa_sink, b_merge, w_proj_gla, w_proj_swa,
             w_out, norm_cross, norm_mem, w_cross_q, w_cross_kv, w_cross_o, norm_moe, w_router, b_router,
             w_up, b_up, w_down, b_down, norm_final):
    p = {}
    lr0 = 2 * GLA_QK_W + 2 * GLA_V_W
    lr1 = lr0 + 2 * GLA_GATE_RANK
    g0 = lr1 + SWA_Q_W + 2 * SWA_KV_W
    p["w_main"] = jnp.concatenate([w_in[:, g0:], w_in[:, :lr0], w_in[:, lr1:g0]], axis=1).astype(BF16)
    p["w_lr"] = jnp.pad(w_in[:, lr0:lr1], ((0, 0), (0, LANES - 2 * GLA_GATE_RANK))).astype(BF16)
    wdec = jnp.zeros((2, LANES, GLA_QK_W), F32)
    for d in range(2):
        wdec = wdec.at[d, d * GLA_GATE_RANK:(d + 1) * GLA_GATE_RANK, :].set(w_gla_decay[d])
    p["w_dec"] = wdec.astype(BF16)
    p["b_dec"] = b_gla_decay.reshape(2, 1, GLA_QK_W)
    p["g_gla"] = gla_out_norm.reshape(1, GLA_V_W)
    p["sink"] = swa_sink
    p["norm_mix"] = norm_mix.reshape(1, D_MODEL)
    p["b_merge"] = b_merge
    p["w_pg"] = w_proj_gla.astype(BF16)
    p["w_ps"] = w_proj_swa.astype(BF16)
    p["w_out"] = w_out.astype(BF16)
    p["norm_cross"] = norm_cross.reshape(1, D_MODEL)
    p["norm_mem"] = norm_mem.reshape(1, D_MODEL)
    p["w_cq"] = w_cross_q.astype(BF16)
    p["w_ckv"] = w_cross_kv.astype(BF16)
    p["w_co"] = w_cross_o.astype(BF16)
    p["norm_moe"] = norm_moe.reshape(1, D_MODEL)
    wr = jnp.pad(w_router, ((0, 0), (0, LANES - N_EXPERTS)))
    wr_hi = wr.astype(BF16)
    p["w_router"] = jnp.stack([wr_hi, (wr - wr_hi.astype(F32)).astype(BF16)])
    p["b_router"] = jnp.pad(b_router, (0, LANES - N_EXPERTS)).reshape(1, LANES)
    p["w_up"] = _deinterleave_up(w_up)
    b_up3 = b_up.reshape(N_EXPERTS, D_FF, 2)
    p["b_up"] = jnp.concatenate([b_up3[:, :, 0], b_up3[:, :, 1]], axis=1).reshape(N_EXPERTS, 1, 2 * D_FF)
    p["w_down"] = w_down.astype(BF16)
    p["b_down"] = jnp.pad(b_down, ((0, LANES - N_EXPERTS), (0, 0))).astype(BF16)
    p["norm_final"] = norm_final.reshape(1, D_MODEL)
    return p


def _mixer_and_cross(x, mem, p):
    bsz, seq, _ = x.shape
    t = bsz * seq
    x2d = x.reshape(t, D_MODEL)
    cos_t, sin_t = _rope_tables(seq)
    proj, lr = _in_projection(x2d, seq, p["norm_mix"], p["w_main"], p["w_lr"], cos_t, sin_t, min(1024, seq))
    proj3 = proj.reshape(bsz, seq, PROJ_COLS)
    heads_per_step = GLA_HEADS if seq <= 2048 else GLA_HEADS // 2
    o_gla = _gla(proj3, lr.reshape(bsz, seq, LANES), p["w_dec"], p["b_dec"], p["g_gla"], heads_per_step)
    o_swa = _swa(proj3, p["sink"], min(512, seq))
    x1 = _merge(x2d, o_gla.reshape(t, GLA_V_W), o_swa.reshape(t, SWA_Q_W), proj, p["b_merge"],
                p["w_pg"], p["w_ps"], p["w_out"], min(512, seq))
    kv = _mem_kv(mem.reshape(bsz * N_MEM, D_MODEL), p["norm_mem"], p["w_ckv"], N_MEM)
    x2 = _cross(x1.reshape(bsz, seq, D_MODEL), kv.reshape(bsz, N_MEM, 2 * CROSS_W), p["norm_cross"],
                p["w_cq"], p["w_co"], min(512, seq))
    return x2.reshape(t, D_MODEL)


def kernel(x_prompt, x_sample, mem_prompt, mem_sample, norm_mix, w_in, w_gla_decay, b_gla_decay, gla_out_norm,
           swa_sink, b_merge, w_proj_gla, w_proj_swa, w_out, norm_cross, norm_mem, w_cross_q, w_cross_kv,
           w_cross_o, norm_moe, w_router, b_router, w_up, b_up, w_down, b_down, norm_final):
    assert norm_mix.shape[0] == 1, "single-layer stack"
    p = _prepare(norm_mix[0], w_in[0], w_gla_decay[0], b_gla_decay[0], gla_out_norm[0], swa_sink[0], b_merge[0],
                 w_proj_gla[0], w_proj_swa[0], w_out[0], norm_cross[0], norm_mem[0], w_cross_q[0],
                 w_cross_kv[0], w_cross_o[0], norm_moe[0], w_router[0], b_router[0], w_up[0], b_up[0],
                 w_down[0], b_down[0], norm_final)
    groups = ((x_prompt, mem_prompt), (x_sample, mem_sample))
    x2 = [_mixer_and_cross(x, mem, p) for x, mem in groups]
    ys = _moe(x2, p)
    return tuple(y.reshape(x.shape) for y, (x, _) in zip(ys, groups))
```

```python
import functools

import jax
import jax.numpy as jnp
from jax import lax
from jax.experimental import pallas as pl
from jax.experimental.pallas import tpu as pltpu

F32 = jnp.float32
BF16 = jnp.bfloat16

D_MODEL = 2048
EPS = 1e-5
N_MEM = 256

GLA_HEADS = 4
GLA_DK = 128
GLA_DV = 256
GLA_QK_W = GLA_HEADS * GLA_DK
GLA_V_W = GLA_HEADS * GLA_DV
GLA_GATE_RANK = 16
GLA_GATE_NORM = 16.0
GLA_CHUNK = 64
GLA_DECAY_BLOCK = 512

SWA_HEADS = 8
SWA_KV_HEADS = 4
SWA_HEAD_DIM = 128
SWA_WINDOW = 128
SWA_Q_W = SWA_HEADS * SWA_HEAD_DIM
SWA_KV_W = SWA_KV_HEADS * SWA_HEAD_DIM
ROPE_THETA = 500000.0
ROPE_DIM = SWA_HEAD_DIM // 4
ROPE_HALF = ROPE_DIM // 2

CROSS_HEADS = 4
CROSS_HEAD_DIM = 128
CROSS_W = CROSS_HEADS * CROSS_HEAD_DIM

N_EXPERTS = 32
TOP_K = 4
D_FF = D_MODEL // 8
SWIGLU_LIMIT = 7.0
SWIGLU_ALPHA = 1.702

LANES = 128

COL_GATES = 0
COL_GQ = 2 * D_MODEL
COL_GK = COL_GQ + GLA_QK_W
COL_GV = COL_GK + GLA_QK_W
COL_GR = COL_GV + GLA_V_W
COL_SQ = COL_GR + GLA_V_W
COL_SK = COL_SQ + SWA_Q_W
COL_SV = COL_SK + SWA_KV_W
PROJ_COLS = COL_SV + SWA_KV_W
PROJ_TN = 1024
J_SQ = COL_SQ // PROJ_TN
J_SKV = COL_SK // PROJ_TN

MIB = 1024 * 1024


def _cparams(semantics, vmem_mib):
    return pltpu.CompilerParams(dimension_semantics=semantics, vmem_limit_bytes=vmem_mib * MIB)


def _rms(xf, g):
    y = xf * lax.rsqrt(jnp.mean(xf * xf, axis=-1, keepdims=True) + EPS)
    return y * g


def _rope(a, partner, cos_ref, sin_ref):
    nh = a.shape[1] // SWA_HEAD_DIM
    cos = jnp.concatenate([cos_ref[...]] * nh, axis=1)
    sin = jnp.concatenate([sin_ref[...]] * nh, axis=1)
    return a * cos + partner * sin


def _inproj_kernel(x_ref, g_ref, w_ref, wlr_ref, cos_ref, sin_ref, swap_ref, proj_ref, lr_ref, h_ref):
    j = pl.program_id(1)

    @pl.when(j == 0)
    def _():
        h = _rms(x_ref[...], g_ref[...]).astype(BF16)
        h_ref[...] = h
        lr_ref[...] = jnp.dot(h, wlr_ref[...], preferred_element_type=F32).astype(BF16)

    acc = jnp.dot(h_ref[...], w_ref[...], preferred_element_type=F32)

    @pl.when(j == J_SQ)
    def _():
        partner = jnp.dot(acc.astype(BF16), swap_ref[...], preferred_element_type=F32)
        proj_ref[...] = _rope(acc, partner, cos_ref, sin_ref).astype(BF16)

    @pl.when(j == J_SKV)
    def _():
        keys = acc[:, :SWA_KV_W]
        partner = jnp.dot(keys.astype(BF16), swap_ref[:SWA_KV_W, :SWA_KV_W], preferred_element_type=F32)
        roped = _rope(keys, partner, cos_ref, sin_ref)
        proj_ref[...] = jnp.concatenate([roped, acc[:, SWA_KV_W:]], axis=1).astype(BF16)

    @pl.when(jnp.logical_and(j != J_SQ, j != J_SKV))
    def _():
        proj_ref[...] = acc.astype(BF16)


def _rope_swap_matrix():
    col = jnp.arange(PROJ_TN, dtype=jnp.int32)
    lane = col % SWA_HEAD_DIM
    src = jnp.where(lane < ROPE_HALF, col + ROPE_HALF, col - ROPE_HALF)
    hit = (col[:, None] == src[None, :]) & (lane[None, :] < ROPE_DIM)
    return hit.astype(BF16)


def _in_projection(x2d, seq, g, w_main, w_lr, cos_t, sin_t, tm):
    t = x2d.shape[0]
    pos_blocks = seq // tm
    swap = _rope_swap_matrix()
    return pl.pallas_call(
        _inproj_kernel,
        grid=(t // tm, PROJ_COLS // PROJ_TN),
        in_specs=[
            pl.BlockSpec((tm, D_MODEL), lambda i, j: (i, 0)),
            pl.BlockSpec((1, D_MODEL), lambda i, j: (0, 0)),
            pl.BlockSpec((D_MODEL, PROJ_TN), lambda i, j: (0, j)),
            pl.BlockSpec((D_MODEL, LANES), lambda i, j: (0, 0)),
            pl.BlockSpec((tm, SWA_HEAD_DIM), lambda i, j: (i % pos_blocks, 0)),
            pl.BlockSpec((tm, SWA_HEAD_DIM), lambda i, j: (i % pos_blocks, 0)),
            pl.BlockSpec((PROJ_TN, PROJ_TN), lambda i, j: (0, 0)),
        ],
        out_specs=[
            pl.BlockSpec((tm, PROJ_TN), lambda i, j: (i, j)),
            pl.BlockSpec((tm, LANES), lambda i, j: (i, 0)),
        ],
        out_shape=[
            jax.ShapeDtypeStruct((t, PROJ_COLS), BF16),
            jax.ShapeDtypeStruct((t, LANES), BF16),
        ],
        scratch_shapes=[pltpu.VMEM((tm, D_MODEL), BF16)],
        compiler_params=_cparams(("parallel", "arbitrary"), 52),
        name="in_projection",
    )(x2d, g, w_main, w_lr, cos_t, sin_t, swap)


def _gla_kernel(q_ref, k_ref, v_ref, r_ref, lr_ref, wdec_ref, bdec_ref, gout_ref, o_ref,
                oacc_ref, state_ref, b_ref):
    seq = q_ref.shape[1]
    heads = q_ref.shape[2] // GLA_DK
    c = GLA_CHUNK
    n_chunks = seq // c
    half = n_chunks // 2
    scale = GLA_DK ** -0.5

    state_ref[...] = jnp.zeros_like(state_ref)

    row = lax.broadcasted_iota(jnp.int32, (c, c), 0)
    col = lax.broadcasted_iota(jnp.int32, (c, c), 1)
    cum_mat = (jnp.where(col <= row, 1.0, 0.0).astype(BF16), jnp.where(col >= row, 1.0, 0.0).astype(BF16))
    att_mask = (col <= row, col > row)

    kw = heads * GLA_DK

    def decay_block(j, carry):
        r0 = pl.multiple_of(j * GLA_DECAY_BLOCK, GLA_DECAY_BLOCK)
        lr = lr_ref[0, pl.ds(r0, GLA_DECAY_BLOCK), :]
        zs = [jnp.dot(lr, wdec_ref[d], preferred_element_type=F32) + bdec_ref[d] for d in range(2)]
        for d in range(2):
            log_a = (jnp.minimum(zs[d], 0.0) - jnp.log(1.0 + jnp.exp(-jnp.abs(zs[d])))) / GLA_GATE_NORM
            la_hi = log_a.astype(BF16)
            la_lo = (log_a - la_hi.astype(F32)).astype(BF16)
            for m in range(GLA_DECAY_BLOCK // c):
                sl = slice(m * c, (m + 1) * c)
                b = (jnp.dot(cum_mat[d], la_hi[sl], preferred_element_type=F32)
                     + jnp.dot(cum_mat[d], la_lo[sl], preferred_element_type=F32))
                b_ref[pl.ds(r0 + m * c, c), d * kw:(d + 1) * kw] = b
        return carry

    lax.fori_loop(0, seq // GLA_DECAY_BLOCK, decay_block, 0)

    def advance(i):
        rows = [pl.ds(pl.multiple_of(n * c, c), c) for n in (i, n_chunks - 1 - i)]
        work = []
        for d in range(2):
            b = b_ref[rows[d], d * kw:(d + 1) * kw]
            b_edge = b[c - 1:c, :] if d == 0 else b[0:1, :]
            q = q_ref[0, rows[d], :].astype(F32) * scale
            k = k_ref[0, rows[d], :].astype(F32)
            q_dec = (q * jnp.exp(b)).astype(BF16)
            k_dec = (k * jnp.exp(-b)).astype(BF16)
            k_end = (k * jnp.exp(b_edge - b)).astype(BF16)
            decay = jnp.exp(b_edge)
            for hh in range(heads):
                kc = slice(hh * GLA_DK, (hh + 1) * GLA_DK)
                att = lax.dot_general(q_dec[:, kc], k_dec[:, kc], (((1,), (1,)), ((), ())),
                                      preferred_element_type=F32)
                att = jnp.where(att_mask[d], att, 0.0).astype(BF16)
                work.append((d, hh, q_dec[:, kc], k_end[:, kc], decay[:, kc], att))
        outs = []
        for d, hh, q_dec, k_end, decay, att in work:
            vcols = slice(hh * GLA_DV, (hh + 1) * GLA_DV)
            v = v_ref[0, rows[d], vcols]
            slot = d * heads + hh
            state = state_ref[slot]
            o = jnp.dot(att, v, preferred_element_type=F32)
            o = o + lax.dot_general(q_dec, state.astype(BF16), (((1,), (1,)), ((), ())),
                                    preferred_element_type=F32)
            kv = lax.dot_general(v, k_end, (((0,), (0,)), ((), ())), preferred_element_type=F32)
            state_ref[slot] = decay * state + kv
            outs.append((rows[d], vcols, o))
        return outs

    def finalize(rows, vcols, o):
        o = o * lax.rsqrt(jnp.mean(o * o, axis=-1, keepdims=True) + EPS) * gout_ref[:, vcols]
        r = r_ref[0, rows, vcols].astype(F32)
        o_ref[0, rows, vcols] = (o * (r * jax.nn.sigmoid(r))).astype(BF16)

    def first_half(i, carry):
        for rows, vcols, o in advance(i):
            oacc_ref[rows, vcols] = o
        return carry

    def second_half(i, carry):
        for rows, vcols, o in advance(i):
            finalize(rows, vcols, oacc_ref[rows, vcols] + o)
        return carry

    lax.fori_loop(0, half, first_half, 0, unroll=4)
    lax.fori_loop(half, n_chunks, second_half, 0, unroll=4)


def _gla(proj3, lr3, wdec_pad, bdec, gout, heads_per_step):
    bsz, seq, _ = proj3.shape
    hp = heads_per_step
    kw, vw = hp * GLA_DK, hp * GLA_DV
    qb, kb = COL_GQ // kw, COL_GK // kw
    vb, rb = COL_GV // vw, COL_GR // vw
    return pl.pallas_call(
        _gla_kernel,
        grid=(bsz, GLA_HEADS // hp),
        in_specs=[
            pl.BlockSpec((1, seq, kw), lambda b, h: (b, 0, qb + h)),
            pl.BlockSpec((1, seq, kw), lambda b, h: (b, 0, kb + h)),
            pl.BlockSpec((1, seq, vw), lambda b, h: (b, 0, vb + h)),
            pl.BlockSpec((1, seq, vw), lambda b, h: (b, 0, rb + h)),
            pl.BlockSpec((1, seq, LANES), lambda b, h: (b, 0, 0)),
            pl.BlockSpec((2, LANES, kw), lambda b, h: (0, 0, h)),
            pl.BlockSpec((2, 1, kw), lambda b, h: (0, 0, h)),
            pl.BlockSpec((1, vw), lambda b, h: (0, h)),
        ],
        out_specs=pl.BlockSpec((1, seq, vw), lambda b, h: (b, 0, h)),
        out_shape=jax.ShapeDtypeStruct((bsz, seq, GLA_V_W), BF16),
        scratch_shapes=[
            pltpu.VMEM((seq, vw), F32),
            pltpu.VMEM((2 * hp, GLA_DV, GLA_DK), F32),
            pltpu.VMEM((seq, 2 * kw), F32),
        ],
        compiler_params=_cparams(("parallel", "parallel"), 56),
        name="gla",
    )(proj3, proj3, proj3, proj3, lr3, wdec_pad, bdec, gout)


def _swa_kernel(sink_ref, q_ref, kp_ref, kc_ref, kn_ref, vp_ref, vc_ref, vn_ref, o_ref, kbuf_ref, vbuf_ref):
    t = pl.program_id(1)
    tq = q_ref.shape[1]
    w = SWA_WINDOW
    hd = SWA_HEAD_DIM
    groups = SWA_HEADS // SWA_KV_HEADS
    seq = pl.num_programs(1) * tq
    scale = hd ** -0.5

    kbuf_ref[0:w, :] = kp_ref[0]
    kbuf_ref[w:w + tq, :] = kc_ref[0]
    kbuf_ref[w + tq:, :] = kn_ref[0]
    vbuf_ref[0:w, :] = vp_ref[0]
    vbuf_ref[w:w + tq, :] = vc_ref[0]
    vbuf_ref[w + tq:, :] = vn_ref[0]

    qi = lax.broadcasted_iota(jnp.int32, (groups * w, 3 * w), 0) & (w - 1)
    kj = lax.broadcasted_iota(jnp.int32, (groups * w, 3 * w), 1)
    in_window = jnp.abs(kj - w - qi) <= w

    def block(m, carry):
        q0 = pl.multiple_of(m * w, w)
        kpos = t * tq + q0 - w + kj
        valid = in_window & (kpos >= 0) & (kpos < seq)
        row_g = lax.broadcasted_iota(jnp.int32, (groups * w, 1), 0) // w
        scores = []
        for kh in range(SWA_KV_HEADS):
            qs = jnp.concatenate(
                [q_ref[0, pl.ds(q0, w), (kh * groups + g) * hd:(kh * groups + g + 1) * hd] for g in range(groups)],
                axis=0)
            ks = kbuf_ref[pl.ds(q0, 3 * w), kh * hd:(kh + 1) * hd]
            scores.append(lax.dot_general(qs, ks, (((1,), (1,)), ((), ())), preferred_element_type=F32))
        probs = []
        for kh in range(SWA_KV_HEADS):
            s = jnp.where(valid, scores[kh] * scale, -jnp.inf)
            sink = jnp.zeros((groups * w, 1), F32)
            for g in range(groups):
                sink = jnp.where(row_g == g, sink_ref[kh * groups + g], sink)
            mx = jnp.maximum(jnp.max(s, axis=-1, keepdims=True), sink)
            p = jnp.exp(s - mx)
            denom = jnp.sum(p, axis=-1, keepdims=True) + jnp.exp(sink - mx)
            probs.append((p.astype(BF16), denom))
        for kh in range(SWA_KV_HEADS):
            p, denom = probs[kh]
            vs = vbuf_ref[pl.ds(q0, 3 * w), kh * hd:(kh + 1) * hd]
            o = jnp.dot(p, vs, preferred_element_type=F32) / denom
            for g in range(groups):
                head = kh * groups + g
                o_ref[0, pl.ds(q0, w), head * hd:(head + 1) * hd] = o[g * w:(g + 1) * w, :].astype(BF16)
        return carry

    lax.fori_loop(0, tq // w, block, 0)


def _swa(proj3, sink, tq):
    bsz, seq, _ = proj3.shape
    w = SWA_WINDOW
    r = tq // w
    last = seq // w - 1
    qb = COL_SQ // SWA_Q_W
    kb, vb = COL_SK // SWA_KV_W, COL_SV // SWA_KV_W

    def neighbours(cb):
        return [
            pl.BlockSpec((1, w, SWA_KV_W), lambda b, t: (b, jnp.maximum(t * r - 1, 0), cb)),
            pl.BlockSpec((1, tq, SWA_KV_W), lambda b, t: (b, t, cb)),
            pl.BlockSpec((1, w, SWA_KV_W), lambda b, t: (b, jnp.minimum((t + 1) * r, last), cb)),
        ]

    return pl.pallas_call(
        _swa_kernel,
        grid=(bsz, seq // tq),
        in_specs=[pl.BlockSpec(memory_space=pltpu.SMEM),
                  pl.BlockSpec((1, tq, SWA_Q_W), lambda b, t: (b, t, qb))] + neighbours(kb) + neighbours(vb),
        out_specs=pl.BlockSpec((1, tq, SWA_Q_W), lambda b, t: (b, t, 0)),
        out_shape=jax.ShapeDtypeStruct((bsz, seq, SWA_Q_W), BF16),
        scratch_shapes=[pltpu.VMEM((tq + 2 * w, SWA_KV_W), BF16), pltpu.VMEM((tq + 2 * w, SWA_KV_W), BF16)],
        compiler_params=_cparams(("parallel", "parallel"), 32),
        name="swa",
    )(sink, proj3, proj3, proj3, proj3, proj3, proj3, proj3)


def _merge_kernel(x_ref, og_ref, os_ref, g0_ref, g1_ref, bm_ref, wg_ref, ws_ref, wo_ref, o_ref):
    a = jnp.dot(og_ref[...], wg_ref[...], preferred_element_type=F32)
    b = jnp.dot(os_ref[...], ws_ref[...], preferred_element_type=F32)
    g0 = jax.nn.sigmoid(g0_ref[...].astype(F32) + bm_ref[0:1, :])
    g1 = jax.nn.sigmoid(g1_ref[...].astype(F32) + bm_ref[1:2, :])
    merged = (g0 * a + g1 * b).astype(BF16)
    o_ref[...] = x_ref[...] + jnp.dot(merged, wo_ref[...], preferred_element_type=F32)


def _resident(shape):
    return pl.BlockSpec(shape, lambda *_: (0,) * len(shape), pipeline_mode=pl.Buffered(1))


def _merge(x2d, o_gla, o_swa, proj, b_merge, w_pg, w_ps, w_out, tm):
    t = x2d.shape[0]
    return pl.pallas_call(
        _merge_kernel,
        grid=(t // tm,),
        in_specs=[
            pl.BlockSpec((tm, D_MODEL), lambda i: (i, 0)),
            pl.BlockSpec((tm, GLA_V_W), lambda i: (i, 0)),
            pl.BlockSpec((tm, SWA_Q_W), lambda i: (i, 0)),
            pl.BlockSpec((tm, D_MODEL), lambda i: (i, 0)),
            pl.BlockSpec((tm, D_MODEL), lambda i: (i, 1)),
            _resident((2, D_MODEL)),
            _resident((GLA_V_W, D_MODEL)),
            _resident((SWA_Q_W, D_MODEL)),
            _resident((D_MODEL, D_MODEL)),
        ],
        out_specs=pl.BlockSpec((tm, D_MODEL), lambda i: (i, 0)),
        out_shape=jax.ShapeDtypeStruct((t, D_MODEL), F32),
        compiler_params=_cparams(("parallel",), 56),
        name="merge",
    )(x2d, o_gla, o_swa, proj, proj, b_merge, w_pg, w_ps, w_out)


def _norm_matmul_kernel(x_ref, g_ref, w_ref, o_ref):
    h = _rms(x_ref[...], g_ref[...]).astype(BF16)
    o_ref[...] = jnp.dot(h, w_ref[...], preferred_element_type=F32).astype(o_ref.dtype)


def _mem_kv(mem2d, g, w_kv, tm):
    t = mem2d.shape[0]
    n = w_kv.shape[1]
    return pl.pallas_call(
        _norm_matmul_kernel,
        grid=(t // tm,),
        in_specs=[
            pl.BlockSpec((tm, D_MODEL), lambda i: (i, 0)),
            _resident((1, D_MODEL)),
            _resident((D_MODEL, n)),
        ],
        out_specs=pl.BlockSpec((tm, n), lambda i: (i, 0)),
        out_shape=jax.ShapeDtypeStruct((t, n), BF16),
        compiler_params=_cparams(("parallel",), 32),
        name="mem_kv",
    )(mem2d, g, w_kv)


def _cross_kernel(x_ref, g_ref, kv_ref, wq_ref, wo_ref, o_ref):
    x = x_ref[0]
    h = _rms(x, g_ref[...]).astype(BF16)
    q = jnp.dot(h, wq_ref[...], preferred_element_type=F32).astype(BF16)
    hd = CROSS_HEAD_DIM
    scale = hd ** -0.5
    scores = [lax.dot_general(q[:, head * hd:(head + 1) * hd], kv_ref[0, :, head * hd:(head + 1) * hd],
                              (((1,), (1,)), ((), ())), preferred_element_type=F32)
              for head in range(CROSS_HEADS)]
    probs = []
    for s in scores:
        s = s * scale
        p = jnp.exp(s - jnp.max(s, axis=-1, keepdims=True))
        probs.append((p.astype(BF16), jnp.sum(p, axis=-1, keepdims=True)))
    outs = []
    for head, (p, denom) in enumerate(probs):
        vh = kv_ref[0, :, CROSS_W + head * hd:CROSS_W + (head + 1) * hd]
        outs.append((jnp.dot(p, vh, preferred_element_type=F32) / denom).astype(BF16))
    o = jnp.concatenate(outs, axis=1)
    o_ref[0] = x + jnp.dot(o, wo_ref[...], preferred_element_type=F32)


def _cross(x3, kv3, g, w_q, w_o, tm):
    bsz, seq, _ = x3.shape
    return pl.pallas_call(
        _cross_kernel,
        grid=(bsz, seq // tm),
        in_specs=[
            pl.BlockSpec((1, tm, D_MODEL), lambda b, i: (b, i, 0)),
            _resident((1, D_MODEL)),
            pl.BlockSpec((1, N_MEM, 2 * CROSS_W), lambda b, i: (b, 0, 0)),
            _resident((D_MODEL, CROSS_W)),
            _resident((CROSS_W, D_MODEL)),
        ],
        out_specs=pl.BlockSpec((1, tm, D_MODEL), lambda b, i: (b, i, 0)),
        out_shape=jax.ShapeDtypeStruct((bsz, seq, D_MODEL), F32),
        compiler_params=_cparams(("parallel", "parallel"), 48),
        name="cross_attention",
    )(x3, g, kv3, w_q, w_o)


HALF_D = D_MODEL // 2
COMBINE_TM = 256
ROUTE_TM = COMBINE_TM
DISPATCH_TM = 512
EXPERT_ROWS = 512


def _pack_rows(a):
    return pltpu.pack_elementwise([a[:, :HALF_D], a[:, HALF_D:]], packed_dtype=BF16)


def _unpack_rows(w):
    lo = pltpu.unpack_elementwise(w, index=0, packed_dtype=BF16, unpacked_dtype=F32)
    hi = pltpu.unpack_elementwise(w, index=1, packed_dtype=BF16, unpacked_dtype=F32)
    return jnp.concatenate([lo, hi], axis=1)


def _route_kernel(x_ref, g_ref, wr_ref, br_ref, cnt0_ref, hp_ref, er_ref, within_ref, comb_ref, cnt_ref,
                  first_ref, tcnt_ref, run_ref):
    i = pl.program_id(0)
    tm = x_ref.shape[0]
    lane = lax.broadcasted_iota(jnp.int32, (tm, LANES), 1)

    @pl.when(i == 0)
    def _():
        run_ref[...] = cnt0_ref[...]

    hf = _rms(x_ref[...], g_ref[...])
    h_hi = hf.astype(BF16)
    h_lo = (hf - h_hi.astype(F32)).astype(BF16)
    both = jnp.dot(h_hi, wr_ref[...], preferred_element_type=F32)
    logits = (both[:, :LANES] + both[:, LANES:]
              + jnp.dot(h_lo, wr_ref[:, :LANES], preferred_element_type=F32)) + br_ref[...]
    logits = jnp.where(lane < N_EXPERTS, logits, -jnp.inf)
    lane_f = lane.astype(F32)
    vals, hots, idxs = [], [], []
    for _ in range(TOP_K):
        mx = jnp.max(logits, axis=-1, keepdims=True)
        idx = jnp.min(jnp.where(logits == mx, lane_f, float(LANES)), axis=-1, keepdims=True)
        hot = lane_f == idx
        vals.append(mx)
        hots.append(hot)
        idxs.append(idx.astype(jnp.int32))
        logits = jnp.where(hot, -jnp.inf, logits)
    exps = [jnp.exp(v - vals[0]) for v in vals]
    denom = exps[0] + exps[1] + exps[2] + exps[3]
    sel = jnp.zeros((tm, LANES), F32)
    comb = jnp.zeros((tm, LANES), F32)
    for k in range(TOP_K):
        sel = sel + jnp.where(hots[k], 1.0, 0.0)
        comb = comb + jnp.where(hots[k], exps[k] / denom, 0.0)
    comb_ref[...] = comb

    row = lax.broadcasted_iota(jnp.int32, (tm, tm), 0)
    col = lax.broadcasted_iota(jnp.int32, (tm, tm), 1)
    earlier = jnp.where(col < row, 1.0, 0.0).astype(BF16)
    within = jnp.dot(earlier, sel.astype(BF16), preferred_element_type=F32)
    rank = within + run_ref[...]
    tile_count = jnp.sum(sel, axis=0, keepdims=True)
    within_ref[...] = within
    first_ref[0] = run_ref[...]
    tcnt_ref[0] = tile_count
    run_ref[...] = run_ref[...] + tile_count
    cnt_ref[...] = run_ref[...]

    table = jnp.zeros((tm, LANES), jnp.int32)
    for k in range(TOP_K):
        rank_k = jnp.sum(jnp.where(hots[k], rank, 0.0), axis=-1, keepdims=True).astype(jnp.int32)
        table = jnp.where(lane == k, idxs[k], table)
        table = jnp.where(lane == TOP_K + k, rank_k, table)
    er_ref[...] = jnp.transpose(table)[:2 * TOP_K, :]
    hp_ref[...] = _pack_rows(hf)


def _route(x2d, g, w_router, b_router, cnt0):
    t = x2d.shape[0]
    tm = ROUTE_TM
    const = lambda *shape: pl.BlockSpec(shape, lambda i: (0,) * len(shape))
    return pl.pallas_call(
        _route_kernel,
        grid=(t // tm,),
        in_specs=[
            pl.BlockSpec((tm, D_MODEL), lambda i: (i, 0)),
            const(1, D_MODEL),
            const(D_MODEL, 2 * LANES),
            const(1, LANES),
            const(1, LANES),
        ],
        out_specs=[
            pl.BlockSpec((tm, HALF_D), lambda i: (i, 0)),
            pl.BlockSpec((2 * TOP_K, tm), lambda i: (0, i)),
            pl.BlockSpec((tm, LANES), lambda i: (i, 0)),
            pl.BlockSpec((tm, LANES), lambda i: (i, 0)),
            const(1, LANES),
            pl.BlockSpec((1, 1, LANES), lambda i: (i, 0, 0)),
            pl.BlockSpec((1, 1, LANES), lambda i: (i, 0, 0)),
        ],
        out_shape=[
            jax.ShapeDtypeStruct((t, HALF_D), jnp.uint32),
            jax.ShapeDtypeStruct((2 * TOP_K, t), jnp.int32),
            jax.ShapeDtypeStruct((t, LANES), F32),
            jax.ShapeDtypeStruct((t, LANES), F32),
            jax.ShapeDtypeStruct((1, LANES), F32),
            jax.ShapeDtypeStruct((t // tm, 1, LANES), F32),
            jax.ShapeDtypeStruct((t // tm, 1, LANES), F32),
        ],
        scratch_shapes=[pltpu.VMEM((1, LANES), F32)],
        compiler_params=_cparams(("arbitrary",), 40),
        name="moe_route",
    )(x2d, g, w_router, b_router, cnt0)


ROW_DMA_UNROLL = 8


def _dispatch_kernel(tv_ref, pos_ref, hp_ref, hs_ref, zero_ref, sem, zero_sem):
    tm = hp_ref.shape[0]

    @pl.when(pl.program_id(0) == 0)
    def _():
        zero_ref[...] = _pack_rows(jnp.zeros((zero_ref.shape[0], D_MODEL), F32))

        def tile_fill(g):
            r0 = pl.multiple_of(g * EXPERT_ROWS, EXPERT_ROWS)
            return pltpu.make_async_copy(zero_ref, hs_ref.at[pl.ds(r0, EXPERT_ROWS), :], zero_sem)

        def start_fill(g, carry):
            @pl.when(tv_ref[g] < EXPERT_ROWS)
            def _():
                tile_fill(g).start()
            return carry

        def wait_fill(g, carry):
            @pl.when(tv_ref[g] < EXPERT_ROWS)
            def _():
                tile_fill(g).wait()
            return carry

        n_tiles = hs_ref.shape[0] // EXPERT_ROWS
        lax.fori_loop(0, n_tiles, start_fill, 0)
        lax.fori_loop(0, n_tiles, wait_fill, 0)

    def row_copy(t, k):
        return pltpu.make_async_copy(hp_ref.at[pl.ds(t, 1), :], hs_ref.at[pl.ds(pos_ref[k, t], 1), :], sem)

    def issue(t, carry):
        for k in range(TOP_K):
            row_copy(t, k).start()
        return carry

    def drain(t, carry):
        for k in range(TOP_K):
            row_copy(t, k).wait()
        return carry

    lax.fori_loop(0, tm, issue, 0, unroll=ROW_DMA_UNROLL)
    lax.fori_loop(0, tm, drain, 0, unroll=ROW_DMA_UNROLL)


def _dispatch(tile_valid, pos, hp, n_rows):
    t = hp.shape[0]
    tm = DISPATCH_TM
    return pl.pallas_call(
        _dispatch_kernel,
        grid=(t // tm,),
        in_specs=[
            pl.BlockSpec(memory_space=pltpu.SMEM),
            pl.BlockSpec((TOP_K, tm), lambda i: (0, i), memory_space=pltpu.SMEM),
            pl.BlockSpec((tm, HALF_D), lambda i: (i, 0)),
        ],
        out_specs=pl.BlockSpec(memory_space=pl.ANY),
        out_shape=jax.ShapeDtypeStruct((n_rows, HALF_D), jnp.uint32),
        scratch_shapes=[pltpu.VMEM((EXPERT_ROWS, HALF_D), jnp.uint32), pltpu.SemaphoreType.DMA,
                        pltpu.SemaphoreType.DMA],
        compiler_params=_cparams(("arbitrary",), 32),
        name="moe_dispatch",
    )(tile_valid, pos, hp)


def _expert_kernel(te_ref, tv_ref, tb_ref, hs_ref, wup_ref, bup_ref, wdn_ref, ys_ref):
    g = pl.program_id(0)
    valid = tv_ref[g]

    @pl.when(valid == 0)
    def _():
        ys_ref[...] = jnp.zeros_like(ys_ref)

    @pl.when(valid > 0)
    def _():
        x = _unpack_rows(hs_ref[...]).astype(BF16)
        up = jnp.dot(x, wup_ref[0], preferred_element_type=F32) + bup_ref[0]
        glu = jnp.minimum(up[:, :D_FF], SWIGLU_LIMIT)
        lin = jnp.clip(up[:, D_FF:], -SWIGLU_LIMIT, SWIGLU_LIMIT)
        act = glu * jax.nn.sigmoid(SWIGLU_ALPHA * glu) * (lin + 1.0)
        y = jnp.dot(act.astype(BF16), wdn_ref[0], preferred_element_type=F32)
        ys_ref[...] = y.astype(BF16)


def _experts(tile_expert, tile_valid, tile_block, hs, w_up, b_up, w_down):
    n_rows = hs.shape[0]
    rows = EXPERT_ROWS
    grid_spec = pltpu.PrefetchScalarGridSpec(
        num_scalar_prefetch=3,
        grid=(n_rows // rows,),
        in_specs=[
            pl.BlockSpec((rows, HALF_D), lambda g, te, tv, tb: (tb[g], 0)),
            pl.BlockSpec((1, D_MODEL, 2 * D_FF), lambda g, te, tv, tb: (te[g], 0, 0)),
            pl.BlockSpec((1, 1, 2 * D_FF), lambda g, te, tv, tb: (te[g], 0, 0)),
            pl.BlockSpec((1, D_FF, D_MODEL), lambda g, te, tv, tb: (te[g], 0, 0)),
        ],
        out_specs=pl.BlockSpec((rows, D_MODEL), lambda g, te, tv, tb: (g, 0)),
    )
    return pl.pallas_call(
        _expert_kernel,
        grid_spec=grid_spec,
        out_shape=jax.ShapeDtypeStruct((n_rows, D_MODEL), BF16),
        compiler_params=_cparams(("arbitrary",), 32),
        name="moe_experts",
    )(tile_expert, tile_valid, tile_block, hs, w_up, b_up, w_down)


COMBINE_WIN = 64
ROW_ALIGN = 16


def _combine_kernel(ws_ref, nx_ref, x_ref, within_ref, comb_ref, shift_ref, expand_ref, bdn_ref, gf_ref, ys_ref,
                    o_ref, win0_ref, win1_ref, extra_ref, acc_ref, sem0, sem1, extra_sem, *, tile_offset):
    i = pl.program_id(0)
    last = pl.num_programs(0) - 1
    tm = x_ref.shape[0]
    w = COMBINE_WIN
    tile = tile_offset + i
    next_tile = tile_offset + jnp.minimum(i + 1, last)

    def step(own_buf, own_sem, spare_buf, spare_sem):
        def window(t, e, buf, sem):
            r0 = pl.multiple_of(ws_ref[t * N_EXPERTS + e], ROW_ALIGN)
            return pltpu.make_async_copy(ys_ref.at[pl.ds(r0, w), :], buf.at[pl.ds(e * w, w), :], sem)

        @pl.when(i == 0)
        def _():
            for e in range(N_EXPERTS):
                window(tile, e, own_buf, own_sem).start()

        for e in range(N_EXPERTS):
            window(next_tile, e, spare_buf, spare_sem).start()

        rel = within_ref[...] + shift_ref[0]
        comb = comb_ref[...]
        rel_b = jnp.dot(rel.astype(BF16), expand_ref[...], preferred_element_type=F32)
        gate_b = jnp.dot(comb.astype(BF16), expand_ref[...], preferred_element_type=F32)
        slot = (lax.broadcasted_iota(jnp.int32, rel_b.shape, 1) & (w - 1)).astype(F32)
        place = jnp.where(rel_b == slot, gate_b, 0.0).astype(BF16)
        acc = x_ref[...] + jnp.dot(comb.astype(BF16), bdn_ref[...], preferred_element_type=F32)

        for e in range(N_EXPERTS):
            window(tile, e, own_buf, own_sem).wait()
        acc_ref[...] = acc + jnp.dot(place, own_buf[...], preferred_element_type=F32)

        def overflow(e, carry):
            more = nx_ref[tile * N_EXPERTS + e]

            @pl.when(more > 0)
            def _():
                lane = lax.broadcasted_iota(jnp.int32, (tm, LANES), 1)
                rel_e = jnp.sum(jnp.where(lane == e, within_ref[...] + shift_ref[0], 0.0), axis=-1, keepdims=True)
                gate_e = jnp.sum(jnp.where(lane == e, comb_ref[...], 0.0), axis=-1, keepdims=True)
                r0 = pl.multiple_of(ws_ref[tile * N_EXPERTS + e], ROW_ALIGN)
                cols = lax.broadcasted_iota(jnp.int32, (tm, w), 1).astype(F32)

                def further(n, inner):
                    cp = pltpu.make_async_copy(ys_ref.at[pl.ds(r0 + n * w, w), :], extra_ref, extra_sem)
                    cp.start()
                    cp.wait()
                    hit = jnp.where(rel_e - (n * w).astype(F32) == cols, gate_e, 0.0).astype(BF16)
                    acc_ref[...] += jnp.dot(hit, extra_ref[...], preferred_element_type=F32)
                    return inner

                lax.fori_loop(1, more + 1, further, 0)

            return carry

        lax.fori_loop(0, N_EXPERTS, overflow, 0)

        o_ref[...] = _rms(acc_ref[...], gf_ref[...])

        @pl.when(i == last)
        def _():
            for e in range(N_EXPERTS):
                window(next_tile, e, spare_buf, spare_sem).wait()

    @pl.when(lax.rem(i, 2) == 0)
    def _():
        step(win0_ref, sem0, win1_ref, sem1)

    @pl.when(lax.rem(i, 2) == 1)
    def _():
        step(win1_ref, sem1, win0_ref, sem0)


def _combine(win_start, win_more, win_shift, tile_offset, x2d, within, comb, expand, b_down, g_final, ys):
    t = x2d.shape[0]
    tm = COMBINE_TM
    const = lambda *shape: pl.BlockSpec(shape, lambda i: (0,) * len(shape))
    smem = pl.BlockSpec(memory_space=pltpu.SMEM)
    return pl.pallas_call(
        functools.partial(_combine_kernel, tile_offset=tile_offset),
        grid=(t // tm,),
        in_specs=[
            smem,
            smem,
            pl.BlockSpec((tm, D_MODEL), lambda i: (i, 0)),
            pl.BlockSpec((tm, LANES), lambda i: (i, 0)),
            pl.BlockSpec((tm, LANES), lambda i: (i, 0)),
            pl.BlockSpec((1, 1, LANES), lambda i: (tile_offset + i, 0, 0)),
            const(LANES, N_EXPERTS * COMBINE_WIN),
            const(LANES, D_MODEL),
            const(1, D_MODEL),
            pl.BlockSpec(memory_space=pl.ANY),
        ],
        out_specs=pl.BlockSpec((tm, D_MODEL), lambda i: (i, 0)),
        out_shape=jax.ShapeDtypeStruct((t, D_MODEL), F32),
        scratch_shapes=[pltpu.VMEM((N_EXPERTS * COMBINE_WIN, D_MODEL), BF16),
                        pltpu.VMEM((N_EXPERTS * COMBINE_WIN, D_MODEL), BF16),
                        pltpu.VMEM((COMBINE_WIN, D_MODEL), BF16),
                        pltpu.VMEM((tm, D_MODEL), F32),
                        pltpu.SemaphoreType.DMA, pltpu.SemaphoreType.DMA, pltpu.SemaphoreType.DMA],
        compiler_params=_cparams(("arbitrary",), 48),
        name="moe_combine",
    )(win_start, win_more, x2d, within, comb, win_shift, expand, b_down, g_final, ys)


def _tile_plan(counts, n_tiles):
    rows = EXPERT_ROWS
    tiles_e = (counts + rows - 1) // rows
    tile_end = jnp.cumsum(tiles_e)
    tile_start = tile_end - tiles_e
    total = tile_end[-1]
    g = jnp.arange(n_tiles, dtype=jnp.int32)
    last = total - 1
    g_eff = jnp.minimum(g, last)
    te = jnp.sum((tile_end[None, :] <= g_eff[:, None]).astype(jnp.int32), axis=1)
    te = jnp.minimum(te, N_EXPERTS - 1)
    valid = jnp.clip(counts[te] - (g_eff - tile_start[te]) * rows, 0, rows)
    valid = jnp.where(g < total, valid, 0).astype(jnp.int32)
    return (tile_start * rows).astype(jnp.int32), te, valid, g_eff.astype(jnp.int32)


def _moe(x2_groups, p):
    n_tok = sum(x.shape[0] for x in x2_groups)
    n_tiles = n_tok * TOP_K // EXPERT_ROWS + N_EXPERTS + 1
    cnt = jnp.zeros((1, LANES), F32)
    routed = []
    for x2d in x2_groups:
        hp, er, within, comb, cnt, first, tile_cnt = _route(x2d, p["norm_moe"], p["w_router"], p["b_router"], cnt)
        routed.append((hp, er, within, comb, first, tile_cnt))
    counts = cnt[0, :N_EXPERTS].astype(jnp.int32)
    group_start, tile_expert, tile_valid, tile_block = _tile_plan(counts, n_tiles)
    hp_all = jnp.concatenate([r[0] for r in routed], axis=0)
    er_all = jnp.concatenate([r[1] for r in routed], axis=1)
    first = jnp.concatenate([r[4] for r in routed], axis=0)[:, 0, :N_EXPERTS].astype(jnp.int32)
    tile_cnt = jnp.concatenate([r[5] for r in routed], axis=0)[:, 0, :N_EXPERTS].astype(jnp.int32)
    seg_start = group_start[None, :] + first
    win_start = (seg_start // ROW_ALIGN) * ROW_ALIGN
    shift = seg_start - win_start
    win_more = jnp.maximum((shift + tile_cnt + COMBINE_WIN - 1) // COMBINE_WIN - 1, 0)
    win_shift = jnp.pad(shift.astype(F32), ((0, 0), (0, LANES - N_EXPERTS)))[:, None, :]
    lane_expert = jnp.arange(N_EXPERTS * COMBINE_WIN, dtype=jnp.int32) // COMBINE_WIN
    expand = (jnp.arange(LANES, dtype=jnp.int32)[:, None] == lane_expert[None, :]).astype(BF16)
    experts = jnp.arange(N_EXPERTS, dtype=jnp.int32)[:, None, None]
    pos = jnp.sum(jnp.where(er_all[None, :TOP_K] == experts, group_start[:, None, None], 0), axis=0) + er_all[TOP_K:]
    hs = _dispatch(tile_valid, pos, hp_all, n_tiles * EXPERT_ROWS)
    ys = _experts(tile_expert, tile_valid, tile_block, hs, p["w_up"], p["b_up"], p["w_down"])
    outs, tile_offset = [], 0
    for x2d, (_, _, within, comb, _, _) in zip(x2_groups, routed):
        outs.append(_combine(win_start.reshape(-1), win_more.reshape(-1), win_shift, tile_offset, x2d, within, comb,
                             expand, p["b_down"], p["norm_final"], ys))
        tile_offset += x2d.shape[0] // COMBINE_TM
    return outs


def _deinterleave_kernel(w_ref, perm_ref, o_ref):
    o_ref[0] = jnp.dot(w_ref[0].astype(BF16), perm_ref[...], preferred_element_type=F32).astype(BF16)


def _deinterleave_up(w_up):
    n_e, d, n = w_up.shape
    src = jnp.arange(n)
    dst = jnp.where(src % 2 == 0, src // 2, n // 2 + src // 2)
    perm = jnp.zeros((n, n), BF16).at[src, dst].set(1.0)
    return pl.pallas_call(
        _deinterleave_kernel,
        grid=(n_e,),
        in_specs=[pl.BlockSpec((1, d, n), lambda e: (e, 0, 0)), pl.BlockSpec((n, n), lambda e: (0, 0))],
        out_specs=pl.BlockSpec((1, d, n), lambda e: (e, 0, 0)),
        out_shape=jax.ShapeDtypeStruct((n_e, d, n), BF16),
        compiler_params=_cparams(("parallel",), 32),
        name="deinterleave_up",
    )(w_up, perm)


def _rope_tables(seq):
    inv = ROPE_THETA ** (-jnp.arange(ROPE_HALF, dtype=F32) / ROPE_HALF)
    ang = jnp.arange(seq, dtype=F32)[:, None] * inv[None, :]
    cos, sin = jnp.cos(ang), jnp.sin(ang)
    pad = SWA_HEAD_DIM - ROPE_DIM
    cos_t = jnp.concatenate([cos, cos, jnp.ones((seq, pad), F32)], axis=1)
    sin_t = jnp.concatenate([-sin, sin, jnp.zeros((seq, pad), F32)], axis=1)
    return cos_t, sin_t


def _prepare(norm_mix, w_in, w_gla_decay, b_gla_decay, gla_out_norm, swa_sink, b_merge, w_proj_gla, w_proj_swa,
             w_out, norm_cross, norm_mem, w_cross_q, w_cross_kv, w_cross_o, norm_moe, w_router, b_router,
             w_up, b_up, w_down, b_down, norm_final):
    p = {}
    lr0 = 2 * GLA_QK_W + 2 * GLA_V_W
    lr1 = lr0 + 2 * GLA_GATE_RANK
    g0 = lr1 + SWA_Q_W + 2 * SWA_KV_W
    p["w_main"] = jnp.concatenate([w_in[:, g0:], w_in[:, :lr0], w_in[:, lr1:g0]], axis=1).astype(BF16)
    p["w_lr"] = jnp.pad(w_in[:, lr0:lr1], ((0, 0), (0, LANES - 2 * GLA_GATE_RANK))).astype(BF16)
    wdec = jnp.zeros((2, LANES, GLA_QK_W), F32)
    for d in range(2):
        wdec = wdec.at[d, d * GLA_GATE_RANK:(d + 1) * GLA_GATE_RANK, :].set(w_gla_decay[d])
    p["w_dec"] = wdec.astype(BF16)
    p["b_dec"] = b_gla_decay.reshape(2, 1, GLA_QK_W)
    p["g_gla"] = gla_out_norm.reshape(1, GLA_V_W)
    p["sink"] = swa_sink
    p["norm_mix"] = norm_mix.reshape(1, D_MODEL)
    p["b_merge"] = b_merge
    p["w_pg"] = w_proj_gla.astype(BF16)
    p["w_ps"] = w_proj_swa.astype(BF16)
    p["w_out"] = w_out.astype(BF16)
    p["norm_cross"] = norm_cross.reshape(1, D_MODEL)
    p["norm_mem"] = norm_mem.reshape(1, D_MODEL)
    p["w_cq"] = w_cross_q.astype(BF16)
    p["w_ckv"] = w_cross_kv.astype(BF16)
    p["w_co"] = w_cross_o.astype(BF16)
    p["norm_moe"] = norm_moe.reshape(1, D_MODEL)
    wr = jnp.pad(w_router, ((0, 0), (0, LANES - N_EXPERTS)))
    wr_hi = wr.astype(BF16)
    p["w_router"] = jnp.concatenate([wr_hi, (wr - wr_hi.astype(F32)).astype(BF16)], axis=1)
    p["b_router"] = jnp.pad(b_router, (0, LANES - N_EXPERTS)).reshape(1, LANES)
    p["w_up"] = _deinterleave_up(w_up)
    b_up3 = b_up.reshape(N_EXPERTS, D_FF, 2)
    p["b_up"] = jnp.concatenate([b_up3[:, :, 0], b_up3[:, :, 1]], axis=1).reshape(N_EXPERTS, 1, 2 * D_FF)
    p["w_down"] = w_down.astype(BF16)
    p["b_down"] = jnp.pad(b_down, ((0, LANES - N_EXPERTS), (0, 0))).astype(BF16)
    p["norm_final"] = norm_final.reshape(1, D_MODEL)
    return p


def _mixer_and_cross(x, mem, p):
    bsz, seq, _ = x.shape
    t = bsz * seq
    x2d = x.reshape(t, D_MODEL)
    cos_t, sin_t = _rope_tables(seq)
    proj, lr = _in_projection(x2d, seq, p["norm_mix"], p["w_main"], p["w_lr"], cos_t, sin_t, min(1024, seq))
    proj3 = proj.reshape(bsz, seq, PROJ_COLS)
    heads_per_step = GLA_HEADS if seq <= 2048 else GLA_HEADS // 2
    o_gla = _gla(proj3, lr.reshape(bsz, seq, LANES), p["w_dec"], p["b_dec"], p["g_gla"], heads_per_step)
    o_swa = _swa(proj3, p["sink"], min(512, seq))
    x1 = _merge(x2d, o_gla.reshape(t, GLA_V_W), o_swa.reshape(t, SWA_Q_W), proj, p["b_merge"],
                p["w_pg"], p["w_ps"], p["w_out"], min(512, seq))
    kv = _mem_kv(mem.reshape(bsz * N_MEM, D_MODEL), p["norm_mem"], p["w_ckv"], N_MEM)
    x2 = _cross(x1.reshape(bsz, seq, D_MODEL), kv.reshape(bsz, N_MEM, 2 * CROSS_W), p["norm_cross"],
                p["w_cq"], p["w_co"], min(512, seq))
    return x2.reshape(t, D_MODEL)


def kernel(x_prompt, x_sample, mem_prompt, mem_sample, norm_mix, w_in, w_gla_decay, b_gla_decay, gla_out_norm,
           swa_sink, b_merge, w_proj_gla, w_proj_swa, w_out, norm_cross, norm_mem, w_cross_q, w_cross_kv,
           w_cross_o, norm_moe, w_router, b_router, w_up, b_up, w_down, b_down, norm_final):
    assert norm_mix.shape[0] == 1, "single-layer stack"
    p = _prepare(norm_mix[0], w_in[0], w_gla_decay[0], b_gla_decay[0], gla_out_norm[0], swa_sink[0], b_merge[0],
                 w_proj_gla[0], w_proj_swa[0], w_out[0], norm_cross[0], norm_mem[0], w_cross_q[0],
                 w_cross_kv[0], w_cross_o[0], norm_moe[0], w_router[0], b_router[0], w_up[0], b_up[0],
                 w_down[0], b_down[0], norm_final)
    groups = ((x_prompt, mem_prompt), (x_sample, mem_sample))
    x2 = [_mixer_and_cross(x, mem, p) for x, mem in groups]
    ys = _moe(x2, p)
    return tuple(y.reshape(x.shape) for y, (x, _) in zip(ys, groups))
```
